```python
import jax, jax.numpy as jnp
from jax import lax
import numpy as np

D_MODEL = 4096
BATCH = 4
SEQ = 4096
DEPTH = 4

GRID_W = 64
CTX_LEN = 256
N_MIXERS = 2
N_NA_LAYERS = (DEPTH + N_MIXERS - 1) // N_MIXERS
N_GLA_LAYERS = DEPTH // N_MIXERS
N_MOD = 9
ADA_RANK = 512
FFN_DIM = 5120
NA_HEAD_DIM = 128
NA_HEADS = D_MODEL // NA_HEAD_DIM
NA_MAX_KH = 8
NA_KW = 16
GLA_HEADS = 8
GLA_DK = D_MODEL // 2 // GLA_HEADS
GLA_DV = D_MODEL // GLA_HEADS
GLA_GATE_RANK = 16
GLA_GATE_NORM = 16.0
GLA_CHUNK = 64
ROPE_BASE = 10000.0
EPS = 1e-6
F32 = jnp.float32

kernel_name = "hybrid_na_gla_macaron_dit"


def rms(x, gain):
    x32 = x.astype(F32)
    y = x32 * lax.rsqrt(jnp.mean(x32 * x32, axis=-1, keepdims=True) + EPS)
    return (y * gain.astype(F32)).astype(x.dtype)


def modulate(h, shift, scale):
    return h * (1.0 + scale[:, None]) + shift[:, None]


def swiglu(h, w_in, w_out):
    a, b = jnp.split(h @ w_in, 2, axis=-1)
    return (jax.nn.silu(a) * b) @ w_out


def axial_rope(x, row, col):
    half = x.shape[-1] // 2
    inv = ROPE_BASE ** (-jnp.arange(0, half, 2, dtype=F32) / half)

    def rot(xa, pos):
        ang = pos.astype(F32)[:, None] * inv[None]
        cos = jnp.cos(ang)[None, :, None, :]
        sin = jnp.sin(ang)[None, :, None, :]
        x1 = xa[..., : half // 2].astype(F32)
        x2 = xa[..., half // 2:].astype(F32)
        return jnp.concatenate([x1 * cos - x2 * sin, x1 * sin + x2 * cos], axis=-1)

    return jnp.concatenate([rot(x[..., :half], row), rot(x[..., half:], col)], axis=-1).astype(x.dtype)


def neighbourhood_attention(h, hc, w_qkv, w_o, rpb, need_ctx):
    B, T, D = h.shape
    L = hc.shape[1]
    H, dh = NA_HEADS, NA_HEAD_DIM
    rows = T // GRID_W
    kh = min(NA_MAX_KH, rows)
    scale = dh ** -0.5
    qkv = (h @ w_qkv).reshape(B, rows, GRID_W, 3, H, dh)
    q = qkv[:, :, :, 0] * scale
    k = qkv[:, :, :, 1]
    v = qkv[:, :, :, 2]
    qkv_c = (hc @ w_qkv).reshape(B, L, 3, H, dh)
    q_c = qkv_c[:, :, 0] * scale
    k_c = qkv_c[:, :, 1]
    v_c = qkv_c[:, :, 2]

    row_start = jnp.asarray(np.clip(np.arange(rows) - kh // 2, 0, rows - kh), jnp.int32)
    col = np.arange(GRID_W)
    col_start = np.clip(col - NA_KW // 2, 0, GRID_W - NA_KW)
    col_mask = jnp.asarray((col[None] >= col_start[:, None]) & (col[None] < col_start[:, None] + NA_KW))
    dc_idx = jnp.asarray(np.clip(col[None] - col[:, None], -(NA_KW - 1), NA_KW - 1) + NA_KW - 1, jnp.int32)
    n_lat = kh * GRID_W

    def row_block(r):
        r0 = row_start[r]
        q_r = lax.dynamic_index_in_dim(q, r, axis=1, keepdims=False)
        k_b = lax.dynamic_slice_in_dim(k, r0, kh, axis=1)
        v_b = lax.dynamic_slice_in_dim(v, r0, kh, axis=1)
        dr = r0 + jnp.arange(kh) - r + (NA_MAX_KH - 1)
        bias = rpb[:, dr[None, :, None], dc_idx[:, None, :]]
        s_lat = jnp.einsum('bqhd,bikhd->bhqik', q_r, k_b).astype(F32) + bias.astype(F32)
        s_lat = jnp.where(col_mask[:, None, :], s_lat, -jnp.inf)
        s_ctx = jnp.einsum('bqhd,blhd->bhql', q_r, k_c).astype(F32)
        s = jnp.concatenate([s_lat.reshape(B, H, GRID_W, n_lat), s_ctx], axis=-1)
        p = jax.nn.softmax(s, axis=-1).astype(v.dtype)
        p_lat = p[..., :n_lat].reshape(B, H, GRID_W, kh, GRID_W)
        p_ctx = p[..., n_lat:]
        return jnp.einsum('bhqik,bikhd->bqhd', p_lat, v_b) + jnp.einsum('bhql,blhd->bqhd', p_ctx, v_c)

    o = lax.map(row_block, jnp.arange(rows))
    out = jnp.moveaxis(o, 0, 1).reshape(B, T, D) @ w_o
    out_c = None
    if need_ctx:
        s_c = jnp.einsum('blhd,bmhd->bhlm', q_c, k_c).astype(F32)
        p_c = jax.nn.softmax(s_c, axis=-1).astype(v_c.dtype)
        out_c = jnp.einsum('bhlm,bmhd->blhd', p_c, v_c).reshape(B, L, D) @ w_o
    return out, out_c


def gla_chunk_scan(q, k, v, g, s0):
    B, T, H, dk = q.shape
    dv = v.shape[-1]
    C = GLA_CHUNK
    n = T // C

    def chunks(a):
        return a.astype(F32).reshape(B, n, C, H, a.shape[-1]).transpose(1, 0, 3, 2, 4)

    tri = jnp.tril(jnp.ones((C, C), bool))

    def step(S, inp):
        qc, kc, vc, gc = inp
        b = jnp.cumsum(gc, axis=2)
        diff = b[:, :, :, None, :] - b[:, :, None, :, :]
        decay = jnp.exp(jnp.where(tri[:, :, None], diff, -jnp.inf))
        a = jnp.sum(qc[:, :, :, None, :] * kc[:, :, None, :, :] * decay, axis=-1)
        o = jnp.einsum('bhtd,bhde->bhte', qc * jnp.exp(b), S) + jnp.einsum('bhts,bhse->bhte', a, vc)
        b_last = b[:, :, -1:, :]
        S_new = jnp.exp(b_last[:, :, 0, :])[..., None] * S + jnp.einsum('bhsd,bhse->bhde', kc * jnp.exp(b_last - b), vc)
        return S_new, o

    S_fin, o = lax.scan(step, s0.astype(F32), (chunks(q), chunks(k), chunks(v), chunks(g)))
    o = o.transpose(1, 0, 3, 2, 4).reshape(B, T, H, dv)
    return o, S_fin


def gated_linear_attention(h, hc, w_in, w_o, g_down, g_up, g_bias, head_gain, need_ctx):
    B, T, D = h.shape
    L = hc.shape[1]
    H, dk, dv = GLA_HEADS, GLA_DK, GLA_DV

    def project(z):
        bz, tz = z.shape[:2]
        p = z @ w_in
        q = p[..., : H * dk].reshape(bz, tz, H, dk) * (dk ** -0.5)
        k = p[..., H * dk: 2 * H * dk].reshape(bz, tz, H, dk)
        v = p[..., 2 * H * dk: 2 * H * dk + H * dv].reshape(bz, tz, H, dv)
        r = p[..., 2 * H * dk + H * dv:]
        g_fwd = (jax.nn.log_sigmoid(((z @ g_down[0]) @ g_up[0] + g_bias[0]).astype(F32)) / GLA_GATE_NORM).reshape(bz, tz, H, dk)
        g_bwd = (jax.nn.log_sigmoid(((z @ g_down[1]) @ g_up[1] + g_bias[1]).astype(F32)) / GLA_GATE_NORM).reshape(bz, tz, H, dk)
        return q, k, v, r, g_fwd, g_bwd

    q, k, v, r, g_f, g_b = project(h)
    t = jnp.arange(T)
    q = axial_rope(q, t // GRID_W, t % GRID_W)
    k = axial_rope(k, t // GRID_W, t % GRID_W)
    q_c, k_c, v_c, r_c, gc_f, gc_b = project(hc)

    flip = lambda a: a[:, ::-1]
    zeros = jnp.zeros((B, H, dk, dv), F32)
    oc_f, s_f = gla_chunk_scan(q_c, k_c, v_c, gc_f, zeros)
    oc_b, s_b = gla_chunk_scan(flip(q_c), flip(k_c), flip(v_c), flip(gc_b), zeros)
    o_f, _ = gla_chunk_scan(q, k, v, g_f, s_f)
    o_b, _ = gla_chunk_scan(flip(q), flip(k), flip(v), flip(g_b), s_b)

    def finish(o, gate):
        bz, tz = o.shape[:2]
        o = rms(o, head_gain).reshape(bz, tz, H * dv)
        return (o * jax.nn.silu(gate.astype(F32))).astype(h.dtype) @ w_o

    out = finish(o_f + flip(o_b), r)
    out_c = finish(oc_f + flip(oc_b), r_c) if need_ctx else None
    return out, out_c


def setup_inputs(seed: int = 0) -> dict:
    key = jax.random.key(seed)
    ks = jax.random.split(key, 20)
    D, F, R = D_MODEL, FFN_DIM, ADA_RANK

    def nrm(k, shape, s):
        return jax.random.normal(k, shape, F32) * s

    return {
        "x": nrm(ks[0], (BATCH, SEQ, D), 1.0),
        "c": nrm(ks[1], (BATCH, D), 1.0),
        "ctx": nrm(ks[2], (BATCH, CTX_LEN, D), 1.0),
        "c_ctx": nrm(ks[3], (D,), 1.0),
        "ada_down": nrm(ks[4], (DEPTH, D, R), D ** -0.5),
        "ada_up": nrm(ks[5], (DEPTH, R, N_MOD * D), 0.5 * R ** -0.5),
        "ada_bias": nrm(ks[6], (DEPTH, N_MOD * D), 0.02),
        "norm_gain": 1.0 + nrm(ks[7], (DEPTH, 3, D), 0.02),
        "ffn_w_in": nrm(ks[8], (DEPTH, 2, D, 2 * F), D ** -0.5),
        "ffn_w_out": nrm(ks[9], (DEPTH, 2, F, D), F ** -0.5),
        "na_w_qkv": nrm(ks[10], (N_NA_LAYERS, D, 3 * D), D ** -0.5),
        "na_w_o": nrm(ks[11], (N_NA_LAYERS, D, D), D ** -0.5),
        "na_rpb": nrm(ks[12], (N_NA_LAYERS, NA_HEADS, 2 * NA_MAX_KH - 1, 2 * NA_KW - 1), 0.1),
        "gla_w_in": nrm(ks[13], (N_GLA_LAYERS, D, 2 * GLA_HEADS * GLA_DK + 2 * GLA_HEADS * GLA_DV), D ** -0.5),
        "gla_w_o": nrm(ks[14], (N_GLA_LAYERS, GLA_HEADS * GLA_DV, D), (GLA_HEADS * GLA_DV) ** -0.5),
        "gla_gate_down": nrm(ks[15], (N_GLA_LAYERS, 2, D, GLA_GATE_RANK), D ** -0.5),
        "gla_gate_up": nrm(ks[16], (N_GLA_LAYERS, 2, GLA_GATE_RANK, GLA_HEADS * GLA_DK), GLA_GATE_RANK ** -0.5),
        "gla_gate_bias": nrm(ks[17], (N_GLA_LAYERS, 2, GLA_HEADS * GLA_DK), 0.1),
        "gla_head_gain": 1.0 + nrm(ks[18], (N_GLA_LAYERS, GLA_DV), 0.02),
        "final_gain": 1.0 + nrm(ks[19], (D,), 0.02),
    }


def reference(x, c, ctx, c_ctx, ada_down, ada_up, ada_bias, norm_gain, ffn_w_in, ffn_w_out,
              na_w_qkv, na_w_o, na_rpb, gla_w_in, gla_w_o, gla_gate_down, gla_gate_up,
              gla_gate_bias, gla_head_gain, final_gain):
    s_lat = jax.nn.silu(c)
    s_ctx = jax.nn.silu(c_ctx)[None]
    xc = ctx
    for i in range(DEPTH):
        last = i == DEPTH - 1
        m = jnp.split((s_lat @ ada_down[i]) @ ada_up[i] + ada_bias[i], N_MOD, axis=-1)
        mc = jnp.split((s_ctx @ ada_down[i]) @ ada_up[i] + ada_bias[i], N_MOD, axis=-1)

        x = x + 0.5 * m[2][:, None] * swiglu(modulate(rms(x, norm_gain[i, 0]), m[0], m[1]), ffn_w_in[i, 0], ffn_w_out[i, 0])
        xc = xc + 0.5 * mc[2][:, None] * swiglu(modulate(rms(xc, norm_gain[i, 0]), mc[0], mc[1]), ffn_w_in[i, 0], ffn_w_out[i, 0])

        hx = modulate(rms(x, norm_gain[i, 1]), m[3], m[4])
        hc = modulate(rms(xc, norm_gain[i, 1]), mc[3], mc[4])
        j = i // N_MIXERS
        if i % N_MIXERS == 0:
            dx, dc = neighbourhood_attention(hx, hc, na_w_qkv[j], na_w_o[j], na_rpb[j], not last)
        else:
            dx, dc = gated_linear_attention(hx, hc, gla_w_in[j], gla_w_o[j], gla_gate_down[j], gla_gate_up[j],
                                            gla_gate_bias[j], gla_head_gain[j], not last)
        x = x + m[5][:, None] * dx

        x = x + 0.5 * m[8][:, None] * swiglu(modulate(rms(x, norm_gain[i, 2]), m[6], m[7]), ffn_w_in[i, 1], ffn_w_out[i, 1])
        if not last:
            xc = xc + mc[5][:, None] * dc
            xc = xc + 0.5 * mc[8][:, None] * swiglu(modulate(rms(xc, norm_gain[i, 2]), mc[6], mc[7]), ffn_w_in[i, 1], ffn_w_out[i, 1])
    return rms(x, final_gain)
```

```python
import functools

import jax
import jax.numpy as jnp
import numpy as np
from jax import lax
from jax.experimental import pallas as pl
from jax.experimental.pallas import tpu as pltpu

F32 = jnp.float32
BF16 = jnp.bfloat16

GRID_W = 64
N_MIXERS = 2
N_MOD = 9
NA_HEAD_DIM = 128
NA_MAX_KH = 8
NA_KW = 16
GLA_HEADS = 8
GLA_GATE_RANK = 16
GLA_GATE_NORM = 16.0
ROPE_BASE = 10000.0
EPS = 1e-6

V7X_LANES = 128
V7X_SUBLANES = 8
V7X_VMEM_BYTES = 64 * 1024 * 1024
VMEM_LIMIT_BYTES = 56 * 1024 * 1024

ROW_TILE = 512
NORM_ROWS = 32
NA_ROW_GROUP = 4
NA_UNION = NA_ROW_GROUP + NA_MAX_KH - 1
GLA_CHUNK = 64
GLA_SEG = 256
GLA_SUB = 8
NEG_BIG = -1e30


def _params(*semantics):
    return pltpu.CompilerParams(dimension_semantics=semantics, vmem_limit_bytes=VMEM_LIMIT_BYTES)


def _sigmoid(x):
    return 1.0 / (1.0 + jnp.exp(-x))


def _ada_kernel(cv_ref, down_ref, up_ref, bias_ref, out_ref, t_scr):
    @pl.when(pl.program_id(1) == 0)
    def _():
        s = cv_ref[...]
        s = s * _sigmoid(s)
        t_scr[...] = jnp.dot(s.astype(BF16), down_ref[0].astype(BF16), preferred_element_type=F32)

    out_ref[0] = jnp.dot(t_scr[...].astype(BF16), up_ref[0].astype(BF16),
                         preferred_element_type=F32) + bias_ref[0]


def _ada_modulation(cvec, ada_down, ada_up, ada_bias):
    depth, d, r = ada_down.shape
    g8 = cvec.shape[0]
    bias = ada_bias.reshape(depth * N_MOD, 1, d)
    out = pl.pallas_call(
        _ada_kernel,
        grid=(depth, N_MOD),
        in_specs=[
            pl.BlockSpec((g8, d), lambda l, k: (0, 0)),
            pl.BlockSpec((1, d, r), lambda l, k: (l, 0, 0)),
            pl.BlockSpec((1, r, d), lambda l, k: (l, 0, k)),
            pl.BlockSpec((1, 1, d), lambda l, k: (l * N_MOD + k, 0, 0)),
        ],
        out_specs=pl.BlockSpec((1, g8, d), lambda l, k: (l * N_MOD + k, 0, 0)),
        out_shape=jax.ShapeDtypeStruct((depth * N_MOD, g8, d), F32),
        scratch_shapes=[pltpu.VMEM((g8, r), F32)],
        compiler_params=_params("arbitrary", "arbitrary"),
        name="ada_modulation",
    )(cvec, ada_down, ada_up, bias)
    return out.reshape(depth, N_MOD, g8, d)


def _normalise_rows(x_ref, gain_ref, shift_ref, scale_ref, h_scr):
    gain = gain_ref[...]
    mul = 1.0 + scale_ref[0]
    shift = shift_ref[0]

    def body(r, carry):
        rows = pl.ds(pl.multiple_of(r * NORM_ROWS, NORM_ROWS), NORM_ROWS)
        x = x_ref[rows, :]
        ms = jnp.mean(x * x, axis=-1, keepdims=True)
        y = x * lax.rsqrt(ms + EPS) * gain
        h_scr[rows, :] = (y * mul + shift).astype(BF16)
        return carry

    lax.fori_loop(0, x_ref.shape[0] // NORM_ROWS, body, 0)


def _nm_swiglu_kernel(x_ref, gain_ref, shift_ref, scale_ref, wa_ref, wb_ref, out_ref, h_scr):
    @pl.when(pl.program_id(1) == 0)
    def _():
        _normalise_rows(x_ref, gain_ref, shift_ref, scale_ref, h_scr)

    h = h_scr[...]
    a = jnp.dot(h, wa_ref[...], preferred_element_type=F32)
    b = jnp.dot(h, wb_ref[...], preferred_element_type=F32)
    out_ref[...] = (a * _sigmoid(a) * b).astype(out_ref.dtype)


def _nm_linear_kernel(x_ref, gain_ref, shift_ref, scale_ref, w_ref, cs_ref, out_ref, h_scr):
    @pl.when(pl.program_id(1) == 0)
    def _():
        _normalise_rows(x_ref, gain_ref, shift_ref, scale_ref, h_scr)

    acc = jnp.dot(h_scr[...], w_ref[...], preferred_element_type=F32)
    out_ref[...] = (acc * cs_ref[...]).astype(out_ref.dtype)


def _group_of_tile(i, tm, lat_rows, seq, n_batch):
    return jnp.where(i * tm < lat_rows, (i * tm) // seq, n_batch)


def _col_tile(n, pref):
    t = min(pref, n)
    while n % t:
        t //= 2
    assert t % V7X_LANES == 0 or t == n, (n, pref)
    return t


def _norm_swiglu(x, gain, shift, scale, w_in_bf16, geom):
    m, d = x.shape
    f = w_in_bf16.shape[1] // 2
    tm = ROW_TILE
    tn = _col_tile(f, 512)
    nf = f // tn
    grp = functools.partial(_group_of_tile, tm=tm, **geom)
    return pl.pallas_call(
        _nm_swiglu_kernel,
        grid=(m // tm, nf),
        in_specs=[
            pl.BlockSpec((tm, d), lambda i, j: (i, 0)),
            pl.BlockSpec((1, d), lambda i, j: (0, 0)),
            pl.BlockSpec((1, 1, d), lambda i, j: (grp(i), 0, 0)),
            pl.BlockSpec((1, 1, d), lambda i, j: (grp(i), 0, 0)),
            pl.BlockSpec((d, tn), lambda i, j: (0, j)),
            pl.BlockSpec((d, tn), lambda i, j: (0, j + nf)),
        ],
        out_specs=pl.BlockSpec((tm, tn), lambda i, j: (i, j)),
        out_shape=jax.ShapeDtypeStruct((m, f), BF16),
        scratch_shapes=[pltpu.VMEM((tm, d), BF16)],
        compiler_params=_params("arbitrary", "arbitrary"),
        name="norm_swiglu",
    )(x, gain, shift, scale, w_in_bf16, w_in_bf16)


def _norm_linear(x, gain, shift, scale, w_bf16, col_scale, out_dtype, geom):
    m, d = x.shape
    n = w_bf16.shape[1]
    tm = ROW_TILE
    tn = _col_tile(n, 1024)
    grp = functools.partial(_group_of_tile, tm=tm, **geom)
    return pl.pallas_call(
        _nm_linear_kernel,
        grid=(m // tm, n // tn),
        in_specs=[
            pl.BlockSpec((tm, d), lambda i, j: (i, 0)),
            pl.BlockSpec((1, d), lambda i, j: (0, 0)),
            pl.BlockSpec((1, 1, d), lambda i, j: (grp(i), 0, 0)),
            pl.BlockSpec((1, 1, d), lambda i, j: (grp(i), 0, 0)),
            pl.BlockSpec((d, tn), lambda i, j: (0, j)),
            pl.BlockSpec((1, tn), lambda i, j: (0, j)),
        ],
        out_specs=pl.BlockSpec((tm, tn), lambda i, j: (i, j)),
        out_shape=jax.ShapeDtypeStruct((m, n), out_dtype),
        scratch_shapes=[pltpu.VMEM((tm, d), BF16)],
        compiler_params=_params("arbitrary", "arbitrary"),
        name="norm_linear",
    )(x, gain, shift, scale, w_bf16, col_scale)


def _mr_kernel(h_ref, w_ref, x_ref, gate_ref, out_ref):
    acc = jnp.dot(h_ref[...], w_ref[...], preferred_element_type=F32)
    out_ref[...] = x_ref[...] + gate_ref[0] * acc


def _matmul_residual(h, w_bf16, x, gate, n_rows, geom):
    k = h.shape[1]
    d = x.shape[1]
    tm = 2 * ROW_TILE
    tn = _col_tile(d, 512)
    grp = functools.partial(_group_of_tile, tm=tm, **geom)
    return pl.pallas_call(
        _mr_kernel,
        grid=(n_rows // tm, d // tn),
        in_specs=[
            pl.BlockSpec((tm, k), lambda i, j: (i, 0)),
            pl.BlockSpec((k, tn), lambda i, j: (0, j)),
            pl.BlockSpec((tm, tn), lambda i, j: (i, j)),
            pl.BlockSpec((1, 1, tn), lambda i, j: (grp(i), 0, j)),
        ],
        out_specs=pl.BlockSpec((tm, tn), lambda i, j: (i, j)),
        out_shape=jax.ShapeDtypeStruct(x.shape, F32),
        input_output_aliases={2: 0},
        compiler_params=_params("arbitrary", "arbitrary"),
        name="matmul_residual",
    )(h, w_bf16, x, gate)


def _rms_kernel(x_ref, gain_ref, out_ref):
    x = x_ref[...]
    ms = jnp.mean(x * x, axis=-1, keepdims=True)
    out_ref[...] = x * lax.rsqrt(ms + EPS) * gain_ref[...]


def _final_rms(x, gain, n_rows):
    d = x.shape[1]
    tm = 256
    return pl.pallas_call(
        _rms_kernel,
        grid=(n_rows // tm,),
        in_specs=[pl.BlockSpec((tm, d), lambda i: (i, 0)), pl.BlockSpec((1, d), lambda i: (0, 0))],
        out_specs=pl.BlockSpec((tm, d), lambda i: (i, 0)),
        out_shape=jax.ShapeDtypeStruct((n_rows, d), F32),
        compiler_params=_params("arbitrary"),
        name="final_rms",
    )(x, gain)


def _na_bias_kernel(rpb_ref, dc_ref, mask_ref, out_ref):
    dc = dc_ref[...]
    rpb = rpb_ref[...]
    acc = jnp.zeros(out_ref.shape, F32)
    for c in range(2 * NA_KW - 1):
        acc = jnp.where(dc == c, rpb[:, c:c + 1], acc)
    out_ref[...] = jnp.where(mask_ref[...] > 0, acc, NEG_BIG)


def _na_bias_tables(rpb, rows):
    n_heads, n_dr, n_dc = rpb.shape
    w = GRID_W
    col = np.arange(w)
    col_start = np.clip(col - NA_KW // 2, 0, w - NA_KW)
    col_mask = (col[None] >= col_start[:, None]) & (col[None] < col_start[:, None] + NA_KW)
    dc_idx = np.clip(col[None] - col[:, None], -(NA_KW - 1), NA_KW - 1) + NA_KW - 1
    rb = V7X_SUBLANES * 5
    n_rows = n_heads * n_dr
    assert n_rows % rb == 0
    toeplitz = pl.pallas_call(
        _na_bias_kernel,
        grid=(n_rows // rb,),
        in_specs=[
            pl.BlockSpec((rb, n_dc), lambda i: (i, 0)),
            pl.BlockSpec((1, w * w), lambda i: (0, 0)),
            pl.BlockSpec((1, w * w), lambda i: (0, 0)),
        ],
        out_specs=pl.BlockSpec((rb, w * w), lambda i: (i, 0)),
        out_shape=jax.ShapeDtypeStruct((n_rows, w * w), F32),
        compiler_params=_params("arbitrary"),
        name="na_bias_toeplitz",
    )(rpb.reshape(n_rows, n_dc), jnp.asarray(dc_idx.reshape(1, -1), jnp.int32),
      jnp.asarray(col_mask.reshape(1, -1), jnp.int32))
    toeplitz = toeplitz.reshape(n_heads, n_dr, w, w)

    g, u, kh = NA_ROW_GROUP, NA_UNION, min(NA_MAX_KH, rows)
    variants = []
    for rg, u0 in ((0, 0), (g, 0), (rows - g, rows - u)):
        blocks = []
        for gi in range(g):
            r = rg + gi
            start = int(np.clip(r - kh // 2, 0, rows - kh))
            row_blocks = []
            for j in range(u):
                key_row = u0 + j
                if start <= key_row < start + kh:
                    row_blocks.append(toeplitz[:, key_row - r + NA_MAX_KH - 1])
                else:
                    row_blocks.append(jnp.full((n_heads, w, w), NEG_BIG, F32))
            blocks.append(jnp.concatenate(row_blocks, axis=-1))
        variants.append(jnp.concatenate(blocks, axis=-2))
    return jnp.stack(variants, axis=1)


def _na_kernel(q_ref, k_ref, v_ref, qc_ref, kc_ref, vc_ref, bias_ref, o_ref, oc_ref, *, rows):
    w = GRID_W
    gq = NA_ROW_GROUP * w
    gk = NA_UNION * w
    n_groups = rows // NA_ROW_GROUP
    nt = (((1,), (1,)), ((), ()))
    kc = kc_ref[...]
    vc = vc_ref[...]

    def body(gi, carry):
        variant = jnp.where(gi == 0, 0, jnp.where(gi == n_groups - 1, 2, 1))
        u0 = jnp.clip(gi * NA_ROW_GROUP - NA_MAX_KH // 2, 0, rows - NA_UNION)
        q_rows = pl.ds(pl.multiple_of(gi * gq, gq), gq)
        k_rows = pl.ds(pl.multiple_of(u0 * w, w), gk)
        qg = q_ref[q_rows, :]
        s_lat = lax.dot_general(qg, k_ref[k_rows, :], nt, preferred_element_type=F32) + bias_ref[0, variant]
        s_ctx = lax.dot_general(qg, kc, nt, preferred_element_type=F32)
        m = jnp.maximum(jnp.max(s_lat, axis=-1, keepdims=True), jnp.max(s_ctx, axis=-1, keepdims=True))
        p_lat = jnp.exp(s_lat - m)
        p_ctx = jnp.exp(s_ctx - m)
        denom = jnp.sum(p_lat, axis=-1, keepdims=True) + jnp.sum(p_ctx, axis=-1, keepdims=True)
        o = jnp.dot(p_lat.astype(BF16), v_ref[k_rows, :], preferred_element_type=F32)
        o = o + jnp.dot(p_ctx.astype(BF16), vc, preferred_element_type=F32)
        o_ref[q_rows, :] = (o / denom).astype(o_ref.dtype)
        return carry

    lax.fori_loop(0, n_groups, body, 0)

    s_c = lax.dot_general(qc_ref[...], kc, nt, preferred_element_type=F32)
    p_c = jnp.exp(s_c - jnp.max(s_c, axis=-1, keepdims=True))
    o_c = jnp.dot(p_c.astype(BF16), vc, preferred_element_type=F32)
    oc_ref[...] = (o_c / jnp.sum(p_c, axis=-1, keepdims=True)).astype(oc_ref.dtype)


def _na_attention(qkv, bias_tables, n_batch, seq, ctx_len):
    d = qkv.shape[1] // 3
    n_heads = d // NA_HEAD_DIM
    dh = NA_HEAD_DIM
    rows = seq // GRID_W
    assert rows % NA_ROW_GROUP == 0 and rows >= NA_UNION
    lat_blocks = n_batch * seq // ctx_len
    lat = lambda part: pl.BlockSpec((seq, dh), lambda b, h: (b, part * n_heads + h))
    ctx = lambda part: pl.BlockSpec((ctx_len, dh), lambda b, h: (lat_blocks + b, part * n_heads + h))
    o_lat, o_ctx = pl.pallas_call(
        functools.partial(_na_kernel, rows=rows),
        grid=(n_batch, n_heads),
        in_specs=[lat(0), lat(1), lat(2), ctx(0), ctx(1), ctx(2),
                  pl.BlockSpec((1,) + bias_tables.shape[1:], lambda b, h: (h, 0, 0, 0))],
        out_specs=[pl.BlockSpec((seq, dh), lambda b, h: (b, h)),
                   pl.BlockSpec((ctx_len, dh), lambda b, h: (b, h))],
        out_shape=[jax.ShapeDtypeStruct((n_batch * seq, d), BF16),
                   jax.ShapeDtypeStruct((n_batch * ctx_len, d), BF16)],
        compiler_params=_params("arbitrary", "arbitrary"),
        name="na_attention",
    )(qkv, qkv, qkv, qkv, qkv, qkv, bias_tables)
    return jnp.concatenate([o_lat, o_ctx], axis=0)


def _split3(x):
    x1 = x.astype(BF16)
    r1 = x - x1.astype(F32)
    x2 = r1.astype(BF16)
    x3 = (r1 - x2.astype(F32)).astype(BF16)
    return x1, x2, x3


def _gla_prep_kernel(q_ref, k_ref, gl_ref, gup_ref, gbias_ref, cos_ref, sin_ref, tril_ref, triu_ref,
                     qo_ref, ko_ref, bf_ref, bb_ref):
    cos = cos_ref[...]
    sin = sin_ref[...]

    def rope(x):
        halves = [pltpu.roll(x[:, s:s + V7X_LANES], V7X_LANES // 2, 1)
                  for s in range(0, x.shape[1], V7X_LANES)]
        return x * cos + jnp.concatenate(halves, axis=1) * sin

    qo_ref[...] = rope(q_ref[...])
    ko_ref[...] = rope(k_ref[...])

    gl = gl_ref[...].astype(BF16)
    tri_rows = tril_ref.shape[0]
    for direction, (tri_ref, out_ref) in enumerate(((tril_ref, bf_ref), (triu_ref, bb_ref))):
        pre = jnp.dot(gl, gup_ref[direction].astype(BF16), preferred_element_type=F32) + gbias_ref[direction]
        g = (jnp.minimum(pre, 0.0) - jnp.log(1.0 + jnp.exp(-jnp.abs(pre)))) * (1.0 / GLA_GATE_NORM)
        tri = tri_ref[...]
        for r0 in range(0, g.shape[0], tri_rows):
            parts = _split3(g[r0:r0 + tri_rows])
            out_ref[r0:r0 + tri_rows, :] = sum(jnp.dot(tri, p, preferred_element_type=F32) for p in parts)


def _gla_prep(qk, gl, gate_up_padded, gate_bias, cos_tab, sin_tab):
    m = qk.shape[0]
    hk = qk.shape[1] // 2
    dk = hk // GLA_HEADS
    tm = ROW_TILE
    tri_rows = 256
    c = GLA_CHUNK
    idx = np.arange(tri_rows)
    same = (idx[:, None] // c) == (idx[None] // c)
    tril = jnp.asarray(same & (idx[None] <= idx[:, None]), BF16)
    triu = jnp.asarray(same & (idx[None] >= idx[:, None]), BF16)
    head = lambda off: pl.BlockSpec((tm, dk), lambda i, h: (i, off + h))
    out_sds = jax.ShapeDtypeStruct((m, hk), F32)
    return pl.pallas_call(
        _gla_prep_kernel,
        grid=(m // tm, GLA_HEADS),
        in_specs=[
            head(0), head(GLA_HEADS),
            pl.BlockSpec((tm, V7X_LANES), lambda i, h: (i, 0)),
            pl.BlockSpec((2, V7X_LANES, dk), lambda i, h: (0, 0, h)),
            pl.BlockSpec((2, 1, dk), lambda i, h: (0, 0, h)),
            pl.BlockSpec((tm, dk), lambda i, h: (i, 0)),
            pl.BlockSpec((tm, dk), lambda i, h: (i, 0)),
            pl.BlockSpec((tri_rows, tri_rows), lambda i, h: (0, 0)),
            pl.BlockSpec((tri_rows, tri_rows), lambda i, h: (0, 0)),
        ],
        out_specs=[head(0)] * 4,
        out_shape=[out_sds] * 4,
        compiler_params=_params("arbitrary", "arbitrary"),
        name="gla_prep",
    )(qk, qk, gl, gate_up_padded, gate_bias, cos_tab, sin_tab, tril, triu)


def _gla_scan_kernel(q_ref, k_ref, b_ref, v_ref, o_ref, st_scr, *, reverse):
    c = GLA_CHUNK
    sub = GLA_SUB
    dk = q_ref.shape[1]
    nt = (((1,), (1,)), ((), ()))
    tn = (((0,), (0,)), ((), ()))

    @pl.when(pl.program_id(2) == 0)
    def _():
        st_scr[...] = jnp.zeros(st_scr.shape, F32)

    row = lax.broadcasted_iota(jnp.int32, (c, c), 0)
    lane = lax.broadcasted_iota(jnp.int32, (c, c), 1)
    t_idx, s_idx = (lane, row) if reverse else (row, lane)
    level_masks = []
    for half in (c // 2, c // 4, c // 8):
        level_masks.append((row // (2 * half) == lane // (2 * half))
                           & (t_idx % (2 * half) >= half) & (s_idx % (2 * half) < half))
    sub_row = lax.broadcasted_iota(jnp.int32, (sub, 1), 0)
    sub_lane = lax.broadcasted_iota(jnp.int32, (sub, c), 1)

    n_chunks = q_ref.shape[0] // c
    order = range(n_chunks - 1, -1, -1) if reverse else range(n_chunks)
    for ci in order:
        rows = slice(ci * c, (ci + 1) * c)
        q = q_ref[rows, :]
        k = k_ref[rows, :]
        b = b_ref[rows, :]
        v = v_ref[rows, :]
        b_end = b[0:1, :] if reverse else b[c - 1:c, :]

        st = st_scr[...]
        o = lax.dot_general((q * jnp.exp(b)).astype(BF16), st.astype(BF16), nt, preferred_element_type=F32)
        k_dec = (k * jnp.exp(b_end - b)).astype(BF16)
        st_scr[...] = st * jnp.exp(b_end) + lax.dot_general(v, k_dec, tn, preferred_element_type=F32)

        a = jnp.zeros((c, c), F32)
        for half, mask in zip((c // 2, c // 4, c // 8), level_masks):
            anchors = []
            for g0 in range(0, c, 2 * half):
                r = g0 + half if reverse else g0 + half - 1
                anchors.append(jnp.broadcast_to(b[r:r + 1, :], (2 * half, dk)))
            anc = anchors[0] if len(anchors) == 1 else jnp.concatenate(anchors, axis=0)
            q_l = (q * jnp.exp(jnp.minimum(b - anc, 0.0))).astype(BF16)
            k_l = (k * jnp.exp(jnp.minimum(anc - b, 0.0))).astype(BF16)
            a = jnp.where(mask, lax.dot_general(q_l, k_l, nt, preferred_element_type=F32), a)

        a_rows = []
        for i in range(c // sub):
            blk = slice(i * sub, (i + 1) * sub)
            q_i = q[blk, :]
            b_i = b[blk, :]
            a_i = a[blk, :]
            for s in range(sub):
                j = i * sub + s
                p = q_i * k[j:j + 1, :] * jnp.exp(jnp.minimum(b_i - b[j:j + 1, :], 0.0))
                col = jnp.sum(p, axis=-1, keepdims=True)
                valid = (sub_row <= s) if reverse else (sub_row >= s)
                a_i = jnp.where((sub_lane == j) & valid, col, a_i)
            a_rows.append(a_i)
        a = jnp.concatenate(a_rows, axis=0)

        o = o + jnp.dot(a.astype(BF16), v, preferred_element_type=F32)
        o_ref[rows, :] = o


def _gla_scan(q, k, bcum, v, v_col0, n_batch, seq, ctx_len, reverse):
    m = q.shape[0]
    dk = q.shape[1] // GLA_HEADS
    dv = 2 * dk
    seg = GLA_SEG
    assert seq % seg == 0 and ctx_len % seg == 0
    n_lat, n_ctx = seq // seg, ctx_len // seg
    lat_blocks = n_batch * n_lat

    def row_block(b, s):
        if reverse:
            return jnp.where(s < n_ctx, lat_blocks + b * n_ctx + (n_ctx - 1 - s),
                             b * n_lat + (n_lat - 1 - (s - n_ctx)))
        return jnp.where(s < n_ctx, lat_blocks + b * n_ctx + s, b * n_lat + (s - n_ctx))

    qk_spec = pl.BlockSpec((seg, dk), lambda b, h, s: (row_block(b, s), h))
    return pl.pallas_call(
        functools.partial(_gla_scan_kernel, reverse=reverse),
        grid=(n_batch, GLA_HEADS, n_ctx + n_lat),
        in_specs=[qk_spec, qk_spec, qk_spec,
                  pl.BlockSpec((seg, dv), lambda b, h, s: (row_block(b, s), v_col0 + h))],
        out_specs=pl.BlockSpec((seg, dv), lambda b, h, s: (row_block(b, s), h)),
        out_shape=jax.ShapeDtypeStruct((m, GLA_HEADS * dv), F32),
        scratch_shapes=[pltpu.VMEM((dv, dk), F32)],
        compiler_params=_params("arbitrary", "arbitrary", "arbitrary"),
        name="gla_scan_bwd" if reverse else "gla_scan_fwd",
    )(q, k, bcum, v)


def _gla_finish_kernel(of_ref, ob_ref, r_ref, gain_ref, out_ref):
    o = of_ref[...] + ob_ref[...]
    ms = jnp.mean(o * o, axis=-1, keepdims=True)
    y = o * lax.rsqrt(ms + EPS) * gain_ref[...]
    r = r_ref[...].astype(F32)
    out_ref[...] = (y * (r * _sigmoid(r))).astype(out_ref.dtype)


def _gla_finish(o_f, o_b, vr, r_col0, head_gain):
    m, hv = o_f.shape
    dv = hv // GLA_HEADS
    tm = ROW_TILE
    spec = pl.BlockSpec((tm, dv), lambda i, h: (i, h))
    return pl.pallas_call(
        _gla_finish_kernel,
        grid=(m // tm, GLA_HEADS),
        in_specs=[spec, spec, pl.BlockSpec((tm, dv), lambda i, h: (i, r_col0 + h)),
                  pl.BlockSpec((1, dv), lambda i, h: (0, 0))],
        out_specs=spec,
        out_shape=jax.ShapeDtypeStruct((m, hv), BF16),
        compiler_params=_params("arbitrary", "arbitrary"),
        name="gla_finish",
    )(o_f, o_b, vr, head_gain)


def _rope_tables(n_batch, seq, ctx_len, dk):
    half = dk // 2
    inv = ROPE_BASE ** (-jnp.arange(0, half, 2, dtype=F32) / half)
    t = jnp.arange(seq)

    def tabs(pos):
        ang = pos.astype(F32)[:, None] * inv[None]
        cos, sin = jnp.cos(ang), jnp.sin(ang)
        return jnp.concatenate([cos, cos], axis=-1), jnp.concatenate([-sin, sin], axis=-1)

    cos_r, sin_r = tabs(t // GRID_W)
    cos_c, sin_c = tabs(t % GRID_W)
    cos = jnp.tile(jnp.concatenate([cos_r, cos_c], axis=-1), (n_batch, 1))
    sin = jnp.tile(jnp.concatenate([sin_r, sin_c], axis=-1), (n_batch, 1))
    n_ctx = n_batch * ctx_len
    cos = jnp.concatenate([cos, jnp.ones((n_ctx, dk), F32)], axis=0)
    sin = jnp.concatenate([sin, jnp.zeros((n_ctx, dk), F32)], axis=0)
    return cos, sin


def kernel(x, c, ctx, c_ctx, ada_down, ada_up, ada_bias, norm_gain, ffn_w_in, ffn_w_out, na_w_qkv, na_w_o, na_rpb, gla_w_in, gla_w_o, gla_gate_down, gla_gate_up, gla_gate_bias, gla_head_gain, final_gain):
    n_batch, seq, d = x.shape
    ctx_len = ctx.shape[1]
    depth = ada_down.shape[0]
    lat_rows = n_batch * seq
    m = lat_rows + n_batch * ctx_len
    geom = dict(lat_rows=lat_rows, seq=seq, n_batch=n_batch)
    assert seq % (2 * ROW_TILE) == 0 and (n_batch * ctx_len) % (2 * ROW_TILE) == 0
    n_groups = n_batch + 1
    g8 = -(-n_groups // V7X_SUBLANES) * V7X_SUBLANES

    xs = jnp.concatenate([x.reshape(lat_rows, d), ctx.reshape(n_batch * ctx_len, d)], axis=0)

    cvec = jnp.concatenate([c, c_ctx[None], jnp.zeros((g8 - n_groups, d), F32)], axis=0)
    mods = _ada_modulation(cvec, ada_down, ada_up, ada_bias)
    mods = mods.reshape(depth, N_MOD, g8, 1, d)

    hk = GLA_HEADS * (d // 2 // GLA_HEADS)
    dk = hk // GLA_HEADS
    ones_d = jnp.ones((1, 3 * d), F32)
    cos_tab, sin_tab = _rope_tables(n_batch, seq, ctx_len, dk)

    for i in range(depth):
        last = i == depth - 1
        mod = mods[i]
        j = i // N_MIXERS

        def ffn(xs, which, n_rows):
            base = 0 if which == 0 else 6
            hid = _norm_swiglu(xs, norm_gain[i, 2 * which][None], mod[base], mod[base + 1],
                               ffn_w_in[i, which].astype(BF16), geom)
            return _matmul_residual(hid, ffn_w_out[i, which].astype(BF16), xs, 0.5 * mod[base + 2],
                                    n_rows, geom)

        xs = ffn(xs, 0, m)

        gain = norm_gain[i, 1][None]
        if i % N_MIXERS == 0:
            col_scale = jnp.concatenate([jnp.full((1, d), NA_HEAD_DIM ** -0.5, F32), jnp.ones((1, 2 * d), F32)], axis=1)
            qkv = _norm_linear(xs, gain, mod[3], mod[4], na_w_qkv[j].astype(BF16), col_scale, BF16, geom)
            tables = _na_bias_tables(na_rpb[j], seq // GRID_W)
            mixed = _na_attention(qkv, tables, n_batch, seq, ctx_len)
            w_o = na_w_o[j]
        else:
            w_in = gla_w_in[j]
            qk_scale = jnp.concatenate([jnp.full((1, hk), dk ** -0.5, F32), jnp.ones((1, hk), F32)], axis=1)
            qk = _norm_linear(xs, gain, mod[3], mod[4], w_in[:, :2 * hk].astype(BF16), qk_scale, F32, geom)
            vr = _norm_linear(xs, gain, mod[3], mod[4], w_in[:, 2 * hk:].astype(BF16), ones_d[:, :2 * d], BF16, geom)
            rank = GLA_GATE_RANK
            gdown = jnp.concatenate([gla_gate_down[j, 0], gla_gate_down[j, 1],
                                     jnp.zeros((d, V7X_LANES - 2 * rank), F32)], axis=1)
            gl = _norm_linear(xs, gain, mod[3], mod[4], gdown.astype(BF16), ones_d[:, :V7X_LANES], F32, geom)
            gup = jnp.zeros((2, V7X_LANES, hk), F32)
            gup = gup.at[0, :rank].set(gla_gate_up[j, 0]).at[1, rank:2 * rank].set(gla_gate_up[j, 1])
            q_r, k_r, b_f, b_b = _gla_prep(qk, gl, gup, gla_gate_bias[j][:, None, :], cos_tab, sin_tab)
            o_f = _gla_scan(q_r, k_r, b_f, vr, 0, n_batch, seq, ctx_len, reverse=False)
            o_b = _gla_scan(q_r, k_r, b_b, vr, 0, n_batch, seq, ctx_len, reverse=True)
            mixed = _gla_finish(o_f, o_b, vr, GLA_HEADS, gla_head_gain[j][None])
            w_o = gla_w_o[j]

        n_rows = lat_rows if last else m
        xs = _matmul_residual(mixed, w_o.astype(BF16), xs, mod[5], n_rows, geom)
        xs = ffn(xs, 1, n_rows)

    return _final_rms(xs, final_gain[None], lat_rows).reshape(n_batch, seq, d)
```

```python
import functools

import jax
import jax.numpy as jnp
import numpy as np
from jax import lax
from jax.experimental import pallas as pl
from jax.experimental.pallas import tpu as pltpu

F32 = jnp.float32
BF16 = jnp.bfloat16

GRID_W = 64
N_MIXERS = 2
N_MOD = 9
NA_HEAD_DIM = 128
NA_MAX_KH = 8
NA_KW = 16
GLA_HEADS = 8
GLA_GATE_RANK = 16
GLA_GATE_NORM = 16.0
ROPE_BASE = 10000.0
EPS = 1e-6

V7X_LANES = 128
V7X_SUBLANES = 8
V7X_VMEM_BYTES = 64 * 1024 * 1024
VMEM_LIMIT_BYTES = 56 * 1024 * 1024

ROW_TILE = 512
NORM_ROWS = 16
NA_ROW_GROUP = 4
NA_UNION = NA_ROW_GROUP + NA_MAX_KH - 1
GLA_CHUNK = 64
GLA_SEG = 256
GLA_SUB = 8
NEG_BIG = -1e30
LOG2_E = 1.4426950408889634


def _params(*semantics):
    return pltpu.CompilerParams(dimension_semantics=semantics, vmem_limit_bytes=VMEM_LIMIT_BYTES)


def _sigmoid(x):
    return 1.0 / (1.0 + jnp.exp(-x))


def _ada_kernel(cv_ref, down_ref, up_ref, bias_ref, out_ref, t_scr):
    @pl.when(pl.program_id(1) == 0)
    def _():
        s = cv_ref[...]
        s = s * _sigmoid(s)
        t_scr[...] = jnp.dot(s.astype(BF16), down_ref[0].astype(BF16), preferred_element_type=F32)

    out_ref[0] = jnp.dot(t_scr[...].astype(BF16), up_ref[0].astype(BF16),
                         preferred_element_type=F32) + bias_ref[0]


def _ada_modulation(cvec, ada_down, ada_up, ada_bias):
    depth, d, r = ada_down.shape
    g8 = cvec.shape[0]
    bias = ada_bias.reshape(depth * N_MOD, 1, d)
    out = pl.pallas_call(
        _ada_kernel,
        grid=(depth, N_MOD),
        in_specs=[
            pl.BlockSpec((g8, d), lambda l, k: (0, 0)),
            pl.BlockSpec((1, d, r), lambda l, k: (l, 0, 0)),
            pl.BlockSpec((1, r, d), lambda l, k: (l, 0, k)),
            pl.BlockSpec((1, 1, d), lambda l, k: (l * N_MOD + k, 0, 0)),
        ],
        out_specs=pl.BlockSpec((1, g8, d), lambda l, k: (l * N_MOD + k, 0, 0)),
        out_shape=jax.ShapeDtypeStruct((depth * N_MOD, g8, d), F32),
        scratch_shapes=[pltpu.VMEM((g8, r), F32)],
        compiler_params=_params("arbitrary", "arbitrary"),
        name="ada_modulation",
    )(cvec, ada_down, ada_up, bias)
    return out.reshape(depth, N_MOD, g8, d)


def _normalise_rows(x_ref, gain_ref, shift_ref, scale_ref, h_scr, gm_scr, sh_scr, rs_scr):
    d = x_ref.shape[1]
    gm_scr[...] = jnp.broadcast_to(gain_ref[...] * (1.0 + scale_ref[0]), (V7X_SUBLANES, d))
    sh_scr[...] = jnp.broadcast_to(shift_ref[0], (V7X_SUBLANES, d))
    reps = NORM_ROWS // V7X_SUBLANES

    n_steps = x_ref.shape[0] // NORM_ROWS

    def row_block(r):
        return pl.ds(pl.multiple_of(r * NORM_ROWS, NORM_ROWS), NORM_ROWS)

    def stats(r, carry):
        x = x_ref[row_block(r), :]
        rs = lax.rsqrt(jnp.mean(x * x, axis=-1, keepdims=True) + EPS)
        rs_scr[row_block(r), :] = jnp.broadcast_to(rs, (NORM_ROWS, V7X_LANES))
        return carry

    def scale(r, carry):
        rows = row_block(r)
        rs = pltpu.repeat(rs_scr[rows, :], d // V7X_LANES, axis=1)
        gm = pltpu.repeat(gm_scr[...], reps, axis=0)
        sh = pltpu.repeat(sh_scr[...], reps, axis=0)
        h_scr[rows, :] = (x_ref[rows, :] * rs * gm + sh).astype(BF16)
        return carry

    lax.fori_loop(0, n_steps, stats, 0, unroll=8)
    lax.fori_loop(0, n_steps, scale, 0, unroll=2)


def _nm_swiglu_kernel(x_ref, gain_ref, shift_ref, scale_ref, wa_ref, wb_ref, out_ref, h_scr, gm_scr, sh_scr, rs_scr):
    @pl.when(pl.program_id(1) == 0)
    def _():
        _normalise_rows(x_ref, gain_ref, shift_ref, scale_ref, h_scr, gm_scr, sh_scr, rs_scr)

    h = h_scr[...]
    a = jnp.dot(h, wa_ref[...], preferred_element_type=F32)
    b = jnp.dot(h, wb_ref[...], preferred_element_type=F32)
    out_ref[...] = (a * _sigmoid(a) * b).astype(out_ref.dtype)


def _nm_linear_kernel(x_ref, gain_ref, shift_ref, scale_ref, w_ref, cs_ref, out_ref, h_scr, gm_scr, sh_scr, rs_scr):
    @pl.when(pl.program_id(1) == 0)
    def _():
        _normalise_rows(x_ref, gain_ref, shift_ref, scale_ref, h_scr, gm_scr, sh_scr, rs_scr)

    acc = jnp.dot(h_scr[...], w_ref[...], preferred_element_type=F32)
    out_ref[...] = (acc * cs_ref[...]).astype(out_ref.dtype)


def _group_of_tile(i, tm, lat_rows, seq, n_batch):
    return jnp.where(i * tm < lat_rows, (i * tm) // seq, n_batch)


def _col_tile(n, pref):
    t = min(pref, n)
    while n % t:
        t //= 2
    assert t % V7X_LANES == 0 or t == n, (n, pref)
    return t


def _weight_spec(lead, rows, tn, col_block0=0):
    return pl.BlockSpec((None,) * len(lead) + (rows, tn), lambda i, j: tuple(lead) + (0, j + col_block0))


def _norm_scratch(tm, d):
    return [pltpu.VMEM((tm, d), BF16), pltpu.VMEM((V7X_SUBLANES, d), F32), pltpu.VMEM((V7X_SUBLANES, d), F32),
            pltpu.VMEM((tm, V7X_LANES), F32)]


def _norm_swiglu(x, gain, shift, scale, w_in, lead, geom):
    m, d = x.shape
    f = w_in.shape[-1] // 2
    tm = ROW_TILE
    tn = _col_tile(f, 512)
    nf = f // tn
    grp = functools.partial(_group_of_tile, tm=tm, **geom)
    return pl.pallas_call(
        _nm_swiglu_kernel,
        grid=(m // tm, nf),
        in_specs=[
            pl.BlockSpec((tm, d), lambda i, j: (i, 0)),
            pl.BlockSpec((1, d), lambda i, j: (0, 0)),
            pl.BlockSpec((1, 1, d), lambda i, j: (grp(i), 0, 0)),
            pl.BlockSpec((1, 1, d), lambda i, j: (grp(i), 0, 0)),
            _weight_spec(lead, d, tn),
            _weight_spec(lead, d, tn, nf),
        ],
        out_specs=pl.BlockSpec((tm, tn), lambda i, j: (i, j)),
        out_shape=jax.ShapeDtypeStruct((m, f), BF16),
        scratch_shapes=_norm_scratch(tm, d),
        compiler_params=_params("arbitrary", "arbitrary"),
        name="norm_swiglu",
    )(x, gain, shift, scale, w_in, w_in)


def _norm_linear(x, gain, shift, scale, w, lead, col0, n, col_scale, out_dtype, geom):
    m, d = x.shape
    tm = ROW_TILE
    tn = _col_tile(n, 1024)
    assert col0 % tn == 0
    grp = functools.partial(_group_of_tile, tm=tm, **geom)
    return pl.pallas_call(
        _nm_linear_kernel,
        grid=(m // tm, n // tn),
        in_specs=[
            pl.BlockSpec((tm, d), lambda i, j: (i, 0)),
            pl.BlockSpec((1, d), lambda i, j: (0, 0)),
            pl.BlockSpec((1, 1, d), lambda i, j: (grp(i), 0, 0)),
            pl.BlockSpec((1, 1, d), lambda i, j: (grp(i), 0, 0)),
            _weight_spec(lead, d, tn, col0 // tn),
            pl.BlockSpec((1, tn), lambda i, j: (0, j)),
        ],
        out_specs=pl.BlockSpec((tm, tn), lambda i, j: (i, j)),
        out_shape=jax.ShapeDtypeStruct((m, n), out_dtype),
        scratch_shapes=_norm_scratch(tm, d),
        compiler_params=_params("arbitrary", "arbitrary"),
        name="norm_linear",
    )(x, gain, shift, scale, w, col_scale)


def _mr_kernel(h_ref, w_ref, x_ref, gate_ref, out_ref):
    acc = jnp.dot(h_ref[...], w_ref[...], preferred_element_type=F32)
    out_ref[...] = x_ref[...] + gate_ref[0] * acc


def _matmul_residual(h, w, lead, x, gate, row0, n_rows, geom):
    k = h.shape[1]
    d = x.shape[1]
    tm = 2 * ROW_TILE
    tn = _col_tile(d, 512)
    assert row0 % tm == 0 and n_rows % tm == 0
    t0 = row0 // tm
    grp = functools.partial(_group_of_tile, tm=tm, **geom)
    return pl.pallas_call(
        _mr_kernel,
        grid=(n_rows // tm, d // tn),
        in_specs=[
            pl.BlockSpec((tm, k), lambda i, j: (i, 0)),
            _weight_spec(lead, k, tn),
            pl.BlockSpec((tm, tn), lambda i, j: (i + t0, j)),
            pl.BlockSpec((1, 1, tn), lambda i, j: (grp(i + t0), 0, j)),
        ],
        out_specs=pl.BlockSpec((tm, tn), lambda i, j: (i + t0, j)),
        out_shape=jax.ShapeDtypeStruct(x.shape, F32),
        input_output_aliases={2: 0},
        compiler_params=_params("arbitrary", "arbitrary"),
        name="matmul_residual",
    )(h, w, x, gate)


def _rms_kernel(x_ref, gain_ref, out_ref):
    x = x_ref[...]
    ms = jnp.mean(x * x, axis=-1, keepdims=True)
    out_ref[...] = x * lax.rsqrt(ms + EPS) * gain_ref[...]


def _final_rms(x, gain, n_rows):
    d = x.shape[1]
    tm = 256
    return pl.pallas_call(
        _rms_kernel,
        grid=(n_rows // tm,),
        in_specs=[pl.BlockSpec((tm, d), lambda i: (i, 0)), pl.BlockSpec((1, d), lambda i: (0, 0))],
        out_specs=pl.BlockSpec((tm, d), lambda i: (i, 0)),
        out_shape=jax.ShapeDtypeStruct((n_rows, d), F32),
        compiler_params=_params("arbitrary"),
        name="final_rms",
    )(x, gain)


def _na_bias_kernel(rpb_ref, dc_ref, mask_ref, out_ref):
    dc = dc_ref[...]
    rpb = rpb_ref[...]
    acc = jnp.zeros(out_ref.shape, F32)
    for c in range(2 * NA_KW - 1):
        acc = jnp.where(dc == c, rpb[:, c:c + 1], acc)
    out_ref[...] = jnp.where(mask_ref[...] > 0, acc, NEG_BIG)


def _na_bias_tables(rpb, rows):
    n_heads, n_dr, n_dc = rpb.shape
    w = GRID_W
    col = np.arange(w)
    col_start = np.clip(col - NA_KW // 2, 0, w - NA_KW)
    col_mask = (col[None] >= col_start[:, None]) & (col[None] < col_start[:, None] + NA_KW)
    dc_idx = np.clip(col[None] - col[:, None], -(NA_KW - 1), NA_KW - 1) + NA_KW - 1
    rb = V7X_SUBLANES * 5
    n_rows = n_heads * n_dr
    assert n_rows % rb == 0
    toeplitz = pl.pallas_call(
        _na_bias_kernel,
        grid=(n_rows // rb,),
        in_specs=[
            pl.BlockSpec((rb, n_dc), lambda i: (i, 0)),
            pl.BlockSpec((1, w * w), lambda i: (0, 0)),
            pl.BlockSpec((1, w * w), lambda i: (0, 0)),
        ],
        out_specs=pl.BlockSpec((rb, w * w), lambda i: (i, 0)),
        out_shape=jax.ShapeDtypeStruct((n_rows, w * w), F32),
        compiler_params=_params("arbitrary"),
        name="na_bias_toeplitz",
    )(rpb.reshape(n_rows, n_dc), jnp.asarray(dc_idx.reshape(1, -1), jnp.int32),
      jnp.asarray(col_mask.reshape(1, -1), jnp.int32))
    toeplitz = toeplitz.reshape(n_heads, n_dr, w, w)

    g, u, kh = NA_ROW_GROUP, NA_UNION, min(NA_MAX_KH, rows)
    variants = []
    for rg, u0 in ((0, 0), (g, 0), (rows - g, rows - u)):
        blocks = []
        for gi in range(g):
            r = rg + gi
            start = int(np.clip(r - kh // 2, 0, rows - kh))
            row_blocks = []
            for j in range(u):
                key_row = u0 + j
                if start <= key_row < start + kh:
                    row_blocks.append(toeplitz[:, key_row - r + NA_MAX_KH - 1])
                else:
                    row_blocks.append(jnp.full((n_heads, w, w), NEG_BIG, F32))
            blocks.append(jnp.concatenate(row_blocks, axis=-1))
        variants.append(jnp.concatenate(blocks, axis=-2))
    return jnp.stack(variants, axis=1)


def _na_kernel(q_ref, k_ref, v_ref, qc_ref, kc_ref, vc_ref, bias_ref, o_ref, oc_ref, *, rows):
    w = GRID_W
    gq = NA_ROW_GROUP * w
    gk = NA_UNION * w
    n_groups = rows // NA_ROW_GROUP
    nt = (((1,), (1,)), ((), ()))
    kc = kc_ref[...]
    vc = vc_ref[...]

    def body(gi, carry):
        variant = jnp.where(gi == 0, 0, jnp.where(gi == n_groups - 1, 2, 1))
        u0 = jnp.clip(gi * NA_ROW_GROUP - NA_MAX_KH // 2, 0, rows - NA_UNION)
        q_rows = pl.ds(pl.multiple_of(gi * gq, gq), gq)
        k_rows = pl.ds(pl.multiple_of(u0 * w, w), gk)
        qg = q_ref[q_rows, :]
        s_lat = lax.dot_general(qg, k_ref[k_rows, :], nt, preferred_element_type=F32) + bias_ref[0, variant]
        s_ctx = lax.dot_general(qg, kc, nt, preferred_element_type=F32)
        m = jnp.maximum(jnp.max(s_lat, axis=-1, keepdims=True), jnp.max(s_ctx, axis=-1, keepdims=True))
        p_lat = jnp.exp(s_lat - m)
        p_ctx = jnp.exp(s_ctx - m)
        denom = jnp.sum(p_lat, axis=-1, keepdims=True) + jnp.sum(p_ctx, axis=-1, keepdims=True)
        o = jnp.dot(p_lat.astype(BF16), v_ref[k_rows, :], preferred_element_type=F32)
        o = o + jnp.dot(p_ctx.astype(BF16), vc, preferred_element_type=F32)
        o_ref[q_rows, :] = (o / denom).astype(o_ref.dtype)
        return carry

    lax.fori_loop(0, n_groups, body, 0, unroll=2)

    s_c = lax.dot_general(qc_ref[...], kc, nt, preferred_element_type=F32)
    p_c = jnp.exp(s_c - jnp.max(s_c, axis=-1, keepdims=True))
    o_c = jnp.dot(p_c.astype(BF16), vc, preferred_element_type=F32)
    oc_ref[...] = (o_c / jnp.sum(p_c, axis=-1, keepdims=True)).astype(oc_ref.dtype)


def _na_attention(qkv, bias_tables, n_batch, seq, ctx_len):
    d = qkv.shape[1] // 3
    n_heads = d // NA_HEAD_DIM
    dh = NA_HEAD_DIM
    rows = seq // GRID_W
    assert rows % NA_ROW_GROUP == 0 and rows >= NA_UNION
    lat_blocks = n_batch * seq // ctx_len
    lat = lambda part: pl.BlockSpec((seq, dh), lambda b, h: (b, part * n_heads + h))
    ctx = lambda part: pl.BlockSpec((ctx_len, dh), lambda b, h: (lat_blocks + b, part * n_heads + h))
    o_lat, o_ctx = pl.pallas_call(
        functools.partial(_na_kernel, rows=rows),
        grid=(n_batch, n_heads),
        in_specs=[lat(0), lat(1), lat(2), ctx(0), ctx(1), ctx(2),
                  pl.BlockSpec((1,) + bias_tables.shape[1:], lambda b, h: (h, 0, 0, 0))],
        out_specs=[pl.BlockSpec((seq, dh), lambda b, h: (b, h)),
                   pl.BlockSpec((ctx_len, dh), lambda b, h: (b, h))],
        out_shape=[jax.ShapeDtypeStruct((n_batch * seq, d), BF16),
                   jax.ShapeDtypeStruct((n_batch * ctx_len, d), BF16)],
        compiler_params=_params("arbitrary", "arbitrary"),
        name="na_attention",
    )(qkv, qkv, qkv, qkv, qkv, qkv, bias_tables)
    return o_lat, o_ctx


def _split3(x):
    x1 = x.astype(BF16)
    r1 = x - x1.astype(F32)
    x2 = r1.astype(BF16)
    x3 = (r1 - x2.astype(F32)).astype(BF16)
    return x1, x2, x3


def _gla_prep_kernel(q_ref, k_ref, gl_ref, gup_ref, gbias_ref, cos_ref, sin_ref, tril_ref, triu_ref,
                     qo_ref, ko_ref, bf_ref, bb_ref):
    cos = cos_ref[...]
    sin = sin_ref[...]

    def rope(x):
        halves = [pltpu.roll(x[:, s:s + V7X_LANES], V7X_LANES // 2, 1)
                  for s in range(0, x.shape[1], V7X_LANES)]
        return x * cos + jnp.concatenate(halves, axis=1) * sin

    qo_ref[...] = rope(q_ref[...])
    ko_ref[...] = rope(k_ref[...])

    gl = gl_ref[...].astype(BF16)
    tri_rows = tril_ref.shape[0]
    for direction, (tri_ref, out_ref) in enumerate(((tril_ref, bf_ref), (triu_ref, bb_ref))):
        pre = jnp.dot(gl, gup_ref[direction].astype(BF16), preferred_element_type=F32) + gbias_ref[direction]
        g = (jnp.minimum(pre, 0.0) - jnp.log(1.0 + jnp.exp(-jnp.abs(pre)))) * (LOG2_E / GLA_GATE_NORM)
        tri = tri_ref[...]
        for r0 in range(0, g.shape[0], tri_rows):
            parts = _split3(g[r0:r0 + tri_rows])
            out_ref[r0:r0 + tri_rows, :] = sum(jnp.dot(tri, p, preferred_element_type=F32) for p in parts)


def _gla_prep(qk, gl, gate_up_padded, gate_bias, cos_tab, sin_tab):
    m = qk.shape[0]
    hk = qk.shape[1] // 2
    dk = hk // GLA_HEADS
    tm = ROW_TILE
    tri_rows = 256
    c = GLA_CHUNK
    idx = np.arange(tri_rows)
    same = (idx[:, None] // c) == (idx[None] // c)
    tril = jnp.asarray(same & (idx[None] <= idx[:, None]), BF16)
    triu = jnp.asarray(same & (idx[None] >= idx[:, None]), BF16)
    head = lambda off: pl.BlockSpec((tm, dk), lambda i, h: (i, off + h))
    out_sds = jax.ShapeDtypeStruct((m, hk), F32)
    return pl.pallas_call(
        _gla_prep_kernel,
        grid=(m // tm, GLA_HEADS),
        in_specs=[
            head(0), head(GLA_HEADS),
            pl.BlockSpec((tm, V7X_LANES), lambda i, h: (i, 0)),
            pl.BlockSpec((2, V7X_LANES, dk), lambda i, h: (0, 0, h)),
            pl.BlockSpec((2, 1, dk), lambda i, h: (0, 0, h)),
            pl.BlockSpec((tm, dk), lambda i, h: (i, 0)),
            pl.BlockSpec((tm, dk), lambda i, h: (i, 0)),
            pl.BlockSpec((tri_rows, tri_rows), lambda i, h: (0, 0)),
            pl.BlockSpec((tri_rows, tri_rows), lambda i, h: (0, 0)),
        ],
        out_specs=[head(0)] * 4,
        out_shape=[out_sds] * 4,
        compiler_params=_params("arbitrary", "arbitrary"),
        name="gla_prep",
    )(qk, qk, gl, gate_up_padded, gate_bias, cos_tab, sin_tab, tril, triu)


def _gla_scan_kernel(q_ref, k_ref, b_ref, v_ref, *rest, reverse, finish):
    if finish:
        other_ref, r_ref, gain_ref, o_ref, st_scr = rest
    else:
        o_ref, st_scr = rest
    c = GLA_CHUNK
    sub = GLA_SUB
    dk = q_ref.shape[1]
    nt = (((1,), (1,)), ((), ()))
    tn = (((0,), (0,)), ((), ()))

    @pl.when(pl.program_id(2) == 0)
    def _():
        st_scr[...] = jnp.zeros(st_scr.shape, F32)

    row = lax.broadcasted_iota(jnp.int32, (c, c), 0)
    lane = lax.broadcasted_iota(jnp.int32, (c, c), 1)
    t_idx, s_idx = (lane, row) if reverse else (row, lane)
    level_masks = []
    for half in (c // 2, c // 4, c // 8):
        level_masks.append((row // (2 * half) == lane // (2 * half))
                           & (t_idx % (2 * half) >= half) & (s_idx % (2 * half) < half))
    sub_row = lax.broadcasted_iota(jnp.int32, (sub, 1), 0)
    sub_lane = lax.broadcasted_iota(jnp.int32, (sub, c), 1)

    n_chunks = q_ref.shape[0] // c
    order = range(n_chunks - 1, -1, -1) if reverse else range(n_chunks)
    for ci in order:
        rows = slice(ci * c, (ci + 1) * c)
        q = q_ref[rows, :]
        k = k_ref[rows, :]
        b = b_ref[rows, :]
        v = v_ref[rows, :]
        b_end = b[0:1, :] if reverse else b[c - 1:c, :]

        st = st_scr[...]
        o = lax.dot_general((q * jnp.exp2(b)).astype(BF16), st.astype(BF16), nt, preferred_element_type=F32)
        k_dec = (k * jnp.exp2(b_end - b)).astype(BF16)
        st_scr[...] = st * jnp.exp2(b_end) + lax.dot_general(v, k_dec, tn, preferred_element_type=F32)

        a = jnp.zeros((c, c), F32)
        for half, mask in zip((c // 2, c // 4, c // 8), level_masks):
            anchors = []
            for g0 in range(0, c, 2 * half):
                r = g0 + half if reverse else g0 + half - 1
                anchors.append(jnp.broadcast_to(b[r:r + 1, :], (2 * half, dk)))
            anc = anchors[0] if len(anchors) == 1 else jnp.concatenate(anchors, axis=0)
            e_l = jnp.exp2(-jnp.abs(b - anc))
            a = jnp.where(mask, lax.dot_general((q * e_l).astype(BF16), (k * e_l).astype(BF16), nt,
                                                preferred_element_type=F32), a)

        a_rows = []
        for i in range(c // sub):
            blk = slice(i * sub, (i + 1) * sub)
            q_i = q[blk, :]
            b_i = b[blk, :]
            a_i = a[blk, :]
            for s in range(sub):
                j = i * sub + s
                p = q_i * k[j:j + 1, :] * jnp.exp2(b_i - b[j:j + 1, :])
                col = jnp.sum(p, axis=-1, keepdims=True)
                valid = (sub_row <= s) if reverse else (sub_row >= s)
                a_i = jnp.where((sub_lane == j) & valid, col, a_i)
            a_rows.append(a_i)
        a = jnp.concatenate(a_rows, axis=0)

        o = o + jnp.dot(a.astype(BF16), v, preferred_element_type=F32)
        if finish:
            o = o + other_ref[rows, :]
            y = o * lax.rsqrt(jnp.mean(o * o, axis=-1, keepdims=True) + EPS) * gain_ref[...]
            r = r_ref[rows, :].astype(F32)
            o = y * (r * _sigmoid(r))
        o_ref[rows, :] = o.astype(o_ref.dtype)


def _gla_scan(q, k, bcum, vr, n_batch, seq, ctx_len, reverse, other=None, head_gain=None):
    finish = other is not None
    m = q.shape[0]
    dk = q.shape[1] // GLA_HEADS
    dv = 2 * dk
    seg = GLA_SEG
    assert seq % seg == 0 and ctx_len % seg == 0
    n_lat, n_ctx = seq // seg, ctx_len // seg
    lat_blocks = n_batch * n_lat

    def row_block(b, s):
        if reverse:
            return jnp.where(s < n_ctx, lat_blocks + b * n_ctx + (n_ctx - 1 - s),
                             b * n_lat + (n_lat - 1 - (s - n_ctx)))
        return jnp.where(s < n_ctx, lat_blocks + b * n_ctx + s, b * n_lat + (s - n_ctx))

    qk_spec = pl.BlockSpec((seg, dk), lambda b, h, s: (row_block(b, s), h))
    v_spec = pl.BlockSpec((seg, dv), lambda b, h, s: (row_block(b, s), h))
    in_specs = [qk_spec, qk_spec, qk_spec, v_spec]
    args = [q, k, bcum, vr]
    if finish:
        in_specs += [v_spec,
                     pl.BlockSpec((seg, dv), lambda b, h, s: (row_block(b, s), GLA_HEADS + h)),
                     pl.BlockSpec((1, dv), lambda b, h, s: (0, 0))]
        args += [other, vr, head_gain]
    return pl.pallas_call(
        functools.partial(_gla_scan_kernel, reverse=reverse, finish=finish),
        grid=(n_batch, GLA_HEADS, n_ctx + n_lat),
        in_specs=in_specs,
        out_specs=v_spec,
        out_shape=jax.ShapeDtypeStruct((m, GLA_HEADS * dv), BF16 if finish else F32),
        scratch_shapes=[pltpu.VMEM((dv, dk), F32)],
        compiler_params=_params("arbitrary", "arbitrary", "arbitrary"),
        name="gla_scan_bwd" if reverse else "gla_scan_fwd",
    )(*args)


def _rope_tables(n_batch, seq, ctx_len, dk):
    half = dk // 2
    inv = ROPE_BASE ** (-jnp.arange(0, half, 2, dtype=F32) / half)
    t = jnp.arange(seq)

    def tabs(pos):
        ang = pos.astype(F32)[:, None] * inv[None]
        cos, sin = jnp.cos(ang), jnp.sin(ang)
        return jnp.concatenate([cos, cos], axis=-1), jnp.concatenate([-sin, sin], axis=-1)

    cos_r, sin_r = tabs(t // GRID_W)
    cos_c, sin_c = tabs(t % GRID_W)
    cos = jnp.tile(jnp.concatenate([cos_r, cos_c], axis=-1), (n_batch, 1))
    sin = jnp.tile(jnp.concatenate([sin_r, sin_c], axis=-1), (n_batch, 1))
    n_ctx = n_batch * ctx_len
    cos = jnp.concatenate([cos, jnp.ones((n_ctx, dk), F32)], axis=0)
    sin = jnp.concatenate([sin, jnp.zeros((n_ctx, dk), F32)], axis=0)
    return cos, sin


def kernel(x, c, ctx, c_ctx, ada_down, ada_up, ada_bias, norm_gain, ffn_w_in, ffn_w_out, na_w_qkv, na_w_o, na_rpb, gla_w_in, gla_w_o, gla_gate_down, gla_gate_up, gla_gate_bias, gla_head_gain, final_gain):
    n_batch, seq, d = x.shape
    ctx_len = ctx.shape[1]
    depth = ada_down.shape[0]
    lat_rows = n_batch * seq
    m = lat_rows + n_batch * ctx_len
    geom = dict(lat_rows=lat_rows, seq=seq, n_batch=n_batch)
    assert seq % (2 * ROW_TILE) == 0 and (n_batch * ctx_len) % (2 * ROW_TILE) == 0
    n_groups = n_batch + 1
    g8 = -(-n_groups // V7X_SUBLANES) * V7X_SUBLANES

    xs = jnp.concatenate([x.reshape(lat_rows, d), ctx.reshape(n_batch * ctx_len, d)], axis=0)

    cvec = jnp.concatenate([c, c_ctx[None], jnp.zeros((g8 - n_groups, d), F32)], axis=0)
    mods = _ada_modulation(cvec, ada_down, ada_up, ada_bias)
    mods = mods.reshape(depth, N_MOD, g8, 1, d)

    hk = GLA_HEADS * (d // 2 // GLA_HEADS)
    dk = hk // GLA_HEADS
    ones_d = jnp.ones((1, 2 * d), F32)
    cos_tab, sin_tab = _rope_tables(n_batch, seq, ctx_len, dk)

    ffn_w_in, ffn_w_out = ffn_w_in.astype(BF16), ffn_w_out.astype(BF16)
    na_w_qkv, na_w_o = na_w_qkv.astype(BF16), na_w_o.astype(BF16)
    gla_w_in, gla_w_o = gla_w_in.astype(BF16), gla_w_o.astype(BF16)

    for i in range(depth):
        last = i == depth - 1
        mod = mods[i]
        j = i // N_MIXERS
        n_rows = lat_rows if last else m

        def ffn(xs, which, n_rows):
            base = 0 if which == 0 else 6
            hid = _norm_swiglu(xs, norm_gain[i, 2 * which][None], mod[base], mod[base + 1],
                               ffn_w_in, (i, which), geom)
            return _matmul_residual(hid, ffn_w_out, (i, which), xs, 0.5 * mod[base + 2], 0, n_rows, geom)

        xs = ffn(xs, 0, m)

        gain = norm_gain[i, 1][None]
        if i % N_MIXERS == 0:
            col_scale = jnp.concatenate([jnp.full((1, d), NA_HEAD_DIM ** -0.5, F32), jnp.ones((1, 2 * d), F32)], axis=1)
            qkv = _norm_linear(xs, gain, mod[3], mod[4], na_w_qkv, (j,), 0, 3 * d, col_scale, BF16, geom)
            tables = _na_bias_tables(na_rpb[j], seq // GRID_W)
            o_lat, o_ctx = _na_attention(qkv, tables, n_batch, seq, ctx_len)
            xs = _matmul_residual(o_lat, na_w_o, (j,), xs, mod[5], 0, lat_rows, geom)
            if not last:
                xs = _matmul_residual(o_ctx, na_w_o, (j,), xs, mod[5], lat_rows, m - lat_rows, geom)
        else:
            qk_scale = jnp.concatenate([jnp.full((1, hk), dk ** -0.5, F32), jnp.ones((1, hk), F32)], axis=1)
            qk = _norm_linear(xs, gain, mod[3], mod[4], gla_w_in, (j,), 0, 2 * hk, qk_scale, F32, geom)
            vr = _norm_linear(xs, gain, mod[3], mod[4], gla_w_in, (j,), 2 * hk, 2 * d, ones_d, BF16, geom)
            rank = GLA_GATE_RANK
            gdown = jnp.concatenate([gla_gate_down[j, 0], gla_gate_down[j, 1],
                                     jnp.zeros((d, V7X_LANES - 2 * rank), F32)], axis=1)
            gl = _norm_linear(xs, gain, mod[3], mod[4], gdown.astype(BF16), (), 0, V7X_LANES,
                              ones_d[:, :V7X_LANES], F32, geom)
            gup = jnp.zeros((2, V7X_LANES, hk), F32)
            gup = gup.at[0, :rank].set(gla_gate_up[j, 0]).at[1, rank:2 * rank].set(gla_gate_up[j, 1])
            q_r, k_r, b_f, b_b = _gla_prep(qk, gl, gup, gla_gate_bias[j][:, None, :], cos_tab, sin_tab)
            o_f = _gla_scan(q_r, k_r, b_f, vr, n_batch, seq, ctx_len, reverse=False)
            mixed = _gla_scan(q_r, k_r, b_b, vr, n_batch, seq, ctx_len, reverse=True,
                              other=o_f, head_gain=gla_head_gain[j][None])
            xs = _matmul_residual(mixed, gla_w_o, (j,), xs, mod[5], 0, n_rows, geom)

        xs = ffn(xs, 1, n_rows)

    return _final_rms(xs, final_gain[None], lat_rows).reshape(n_batch, seq, d)
```

```python
import functools

import jax
import jax.numpy as jnp
import numpy as np
from jax import lax
from jax.experimental import pallas as pl
from jax.experimental.pallas import tpu as pltpu

F32 = jnp.float32
BF16 = jnp.bfloat16

GRID_W = 64
N_MIXERS = 2
N_MOD = 9
NA_HEAD_DIM = 128
NA_MAX_KH = 8
NA_KW = 16
GLA_HEADS = 8
GLA_GATE_RANK = 16
GLA_GATE_NORM = 16.0
ROPE_BASE = 10000.0
EPS = 1e-6

V7X_LANES = 128
V7X_SUBLANES = 8
V7X_VMEM_BYTES = 64 * 1024 * 1024
VMEM_LIMIT_BYTES = 56 * 1024 * 1024

ROW_TILE = 512
NORM_ROWS = 16
NA_ROW_GROUP = 4
NA_UNION = NA_ROW_GROUP + NA_MAX_KH - 1
GLA_CHUNK = 64
GLA_SEG = 256
GLA_SUB = 8
NEG_BIG = -1e30
LOG2_E = 1.4426950408889634


def _params(*semantics):
    return pltpu.CompilerParams(dimension_semantics=semantics, vmem_limit_bytes=VMEM_LIMIT_BYTES)


def _sigmoid(x):
    return 1.0 / (1.0 + jnp.exp(-x))


def _ada_kernel(cv_ref, down_ref, up_ref, bias_ref, out_ref, t_scr):
    @pl.when(pl.program_id(1) == 0)
    def _():
        s = cv_ref[...]
        s = s * _sigmoid(s)
        t_scr[...] = jnp.dot(s.astype(BF16), down_ref[0].astype(BF16), preferred_element_type=F32)

    out_ref[0] = jnp.dot(t_scr[...].astype(BF16), up_ref[0].astype(BF16),
                         preferred_element_type=F32) + bias_ref[0]


def _ada_modulation(cvec, ada_down, ada_up, ada_bias):
    depth, d, r = ada_down.shape
    g8 = cvec.shape[0]
    bias = ada_bias.reshape(depth * N_MOD, 1, d)
    out = pl.pallas_call(
        _ada_kernel,
        grid=(depth, N_MOD),
        in_specs=[
            pl.BlockSpec((g8, d), lambda l, k: (0, 0)),
            pl.BlockSpec((1, d, r), lambda l, k: (l, 0, 0)),
            pl.BlockSpec((1, r, d), lambda l, k: (l, 0, k)),
            pl.BlockSpec((1, 1, d), lambda l, k: (l * N_MOD + k, 0, 0)),
        ],
        out_specs=pl.BlockSpec((1, g8, d), lambda l, k: (l * N_MOD + k, 0, 0)),
        out_shape=jax.ShapeDtypeStruct((depth * N_MOD, g8, d), F32),
        scratch_shapes=[pltpu.VMEM((g8, r), F32)],
        compiler_params=_params("arbitrary", "arbitrary"),
        name="ada_modulation",
    )(cvec, ada_down, ada_up, bias)
    return out.reshape(depth, N_MOD, g8, d)


def _group_of_tile(i, tm, lat_rows, seq, n_batch):
    return jnp.where(i * tm < lat_rows, (i * tm) // seq, n_batch)


def _col_tile(n, pref):
    t = min(pref, n)
    while n % t:
        t //= 2
    assert t % V7X_LANES == 0 or t == n, (n, pref)
    return t


def _weight_spec(lead, rows, tn, col_block0=0):
    return pl.BlockSpec((None,) * len(lead) + (rows, tn), lambda i, j: tuple(lead) + (0, j + col_block0))


def _pipe_decode(t, ns, nj, n_tiles):
    warm = t < ns
    u = jnp.maximum(t - ns, 0)
    i, j = u // nj, u % nj
    norm_tile = jnp.where(warm, 0, jnp.minimum(i + 1, n_tiles - 1))
    norm_slice = jnp.where(warm, t, jnp.minimum(j, ns - 1))
    return i, j, norm_tile, norm_slice


def _pnm_kernel(x_ref, gain_ref, shift_ref, scale_ref, *rest, mode, ns, nj, n_tiles, has_extra):
    if mode == "swiglu":
        wa_ref, wb_ref, out_ref, h_scr, gm_scr, sh_scr = rest
    elif has_extra:
        w_ref, cs_ref, we_ref, out_ref, oute_ref, h_scr, gm_scr, sh_scr = rest
    else:
        w_ref, cs_ref, out_ref, h_scr, gm_scr, sh_scr = rest
    t = pl.program_id(0)
    i, j, _, s = _pipe_decode(t, ns, nj, n_tiles)
    running = t >= ns
    d = x_ref.shape[1]
    sr = x_ref.shape[0]

    @pl.when(jnp.logical_or(t == 0, jnp.logical_and(running, j == 0)))
    def _():
        gm_scr[...] = jnp.broadcast_to(gain_ref[...] * (1.0 + scale_ref[0]), (V7X_SUBLANES, d))
        sh_scr[...] = jnp.broadcast_to(shift_ref[0], (V7X_SUBLANES, d))

    def normalise_slice(h_dst):
        reps = NORM_ROWS // V7X_SUBLANES
        gm = jnp.concatenate([gm_scr[...]] * reps, axis=0)
        sh = jnp.concatenate([sh_scr[...]] * reps, axis=0)
        for r0 in range(0, sr, NORM_ROWS):
            x = x_ref[r0:r0 + NORM_ROWS, :]
            rs = lax.rsqrt(jnp.mean(x * x, axis=-1, keepdims=True) + EPS)
            rows = pl.ds(pl.multiple_of(s * sr + r0, NORM_ROWS), NORM_ROWS)
            h_dst[rows, :] = (x * rs * gm + sh).astype(BF16)

    def step(parity, with_extra):
        h_first = h_scr[parity]
        if mode == "swiglu":
            a = jnp.dot(h_first, wa_ref[...], preferred_element_type=F32)
        else:
            half = w_ref.shape[1] // 2
            a = jnp.dot(h_first, w_ref[:, :half], preferred_element_type=F32)
        normalise_slice(h_scr.at[1 - parity])
        h_second = h_scr[i % 2]
        if mode == "swiglu":
            b = jnp.dot(h_second, wb_ref[...], preferred_element_type=F32)
            out_ref[...] = (a * _sigmoid(a) * b).astype(out_ref.dtype)
        else:
            b = jnp.dot(h_second, w_ref[:, half:], preferred_element_type=F32)
            out_ref[:, :half] = (a * cs_ref[:, :half]).astype(out_ref.dtype)
            out_ref[:, half:] = (b * cs_ref[:, half:]).astype(out_ref.dtype)
        if with_extra:
            oute_ref[...] = jnp.dot(h_second, we_ref[...], preferred_element_type=F32)

    @pl.when(jnp.logical_not(running))
    def _():
        normalise_slice(h_scr.at[0])

    for parity in (0, 1):
        here = jnp.logical_and(running, i % 2 == parity)
        if has_extra:
            pl.when(jnp.logical_and(here, j == 0))(functools.partial(step, parity, True))
            pl.when(jnp.logical_and(here, j > 0))(functools.partial(step, parity, False))
        else:
            pl.when(here)(functools.partial(step, parity, False))


def _pipelined_norm_matmul(x, gain, shift, scale, mode, weights, n, tn, col_scale, out_dtype, extra, geom):
    m, d = x.shape
    tm = 2 * ROW_TILE
    n_tiles, nj = m // tm, n // tn
    ns = 1
    while 2 * ns <= min(nj, 8):
        ns *= 2
    sr = tm // ns
    assert sr % NORM_ROWS == 0
    dec = functools.partial(_pipe_decode, ns=ns, nj=nj, n_tiles=n_tiles)
    grp = functools.partial(_group_of_tile, tm=tm, **geom)
    mod_spec = pl.BlockSpec((1, 1, d), lambda t: (grp(dec(t)[2]), 0, 0))
    in_specs = [
        pl.BlockSpec((sr, d), lambda t: (dec(t)[2] * ns + dec(t)[3], 0)),
        pl.BlockSpec((1, d), lambda t: (0, 0)),
        mod_spec, mod_spec,
    ]
    args = [x, gain, shift, scale]
    for w, lead, col_block0 in weights:
        in_specs.append(pl.BlockSpec((None,) * len(lead) + (d, tn),
                                     lambda t, lead=lead, c0=col_block0: tuple(lead) + (0, dec(t)[1] + c0)))
        args.append(w)
    out_spec = pl.BlockSpec((tm, tn), lambda t: (dec(t)[0], dec(t)[1]))
    out_shape = jax.ShapeDtypeStruct((m, n), out_dtype)
    if mode == "linear":
        in_specs.append(pl.BlockSpec((1, tn), lambda t: (0, dec(t)[1])))
        args.append(col_scale)
    if extra is not None:
        e = extra.shape[1]
        in_specs.append(pl.BlockSpec((d, e), lambda t: (0, 0)))
        args.append(extra)
        out_spec = [out_spec, pl.BlockSpec((tm, e), lambda t: (dec(t)[0], 0))]
        out_shape = [out_shape, jax.ShapeDtypeStruct((m, e), F32)]
    return pl.pallas_call(
        functools.partial(_pnm_kernel, mode=mode, ns=ns, nj=nj, n_tiles=n_tiles, has_extra=extra is not None),
        grid=(ns + n_tiles * nj,),
        in_specs=in_specs,
        out_specs=out_spec,
        out_shape=out_shape,
        scratch_shapes=[pltpu.VMEM((2, tm, d), BF16),
                        pltpu.VMEM((V7X_SUBLANES, d), F32), pltpu.VMEM((V7X_SUBLANES, d), F32)],
        compiler_params=_params("arbitrary"),
        name="norm_" + mode,
    )(*args)


def _mr_kernel(h_ref, w_ref, x_ref, gate_ref, out_ref):
    acc = jnp.dot(h_ref[...], w_ref[...], preferred_element_type=F32)
    out_ref[...] = x_ref[...] + gate_ref[0] * acc


def _matmul_residual(h, w, lead, x, gate, row0, n_rows, geom):
    k = h.shape[1]
    d = x.shape[1]
    tm = 2 * ROW_TILE
    tn = _col_tile(d, 512)
    assert row0 % tm == 0 and n_rows % tm == 0
    t0 = row0 // tm
    grp = functools.partial(_group_of_tile, tm=tm, **geom)
    return pl.pallas_call(
        _mr_kernel,
        grid=(n_rows // tm, d // tn),
        in_specs=[
            pl.BlockSpec((tm, k), lambda i, j: (i, 0)),
            _weight_spec(lead, k, tn),
            pl.BlockSpec((tm, tn), lambda i, j: (i + t0, j)),
            pl.BlockSpec((1, 1, tn), lambda i, j: (grp(i + t0), 0, j)),
        ],
        out_specs=pl.BlockSpec((tm, tn), lambda i, j: (i + t0, j)),
        out_shape=jax.ShapeDtypeStruct(x.shape, F32),
        input_output_aliases={2: 0},
        compiler_params=_params("arbitrary", "arbitrary"),
        name="matmul_residual",
    )(h, w, x, gate)


def _rms_kernel(x_ref, gain_ref, out_ref):
    x = x_ref[...]
    ms = jnp.mean(x * x, axis=-1, keepdims=True)
    out_ref[...] = x * lax.rsqrt(ms + EPS) * gain_ref[...]


def _final_rms(x, gain, n_rows):
    d = x.shape[1]
    tm = 256
    return pl.pallas_call(
        _rms_kernel,
        grid=(n_rows // tm,),
        in_specs=[pl.BlockSpec((tm, d), lambda i: (i, 0)), pl.BlockSpec((1, d), lambda i: (0, 0))],
        out_specs=pl.BlockSpec((tm, d), lambda i: (i, 0)),
        out_shape=jax.ShapeDtypeStruct((n_rows, d), F32),
        compiler_params=_params("arbitrary"),
        name="final_rms",
    )(x, gain)


def _na_bias_kernel(rpb_ref, dc_ref, mask_ref, out_ref):
    dc = dc_ref[...]
    rpb = rpb_ref[...]
    acc = jnp.zeros(out_ref.shape, F32)
    for c in range(2 * NA_KW - 1):
        acc = jnp.where(dc == c, rpb[:, c:c + 1], acc)
    out_ref[...] = jnp.where(mask_ref[...] > 0, acc, NEG_BIG)


def _na_bias_tables(rpb, rows):
    n_heads, n_dr, n_dc = rpb.shape
    w = GRID_W
    col = np.arange(w)
    col_start = np.clip(col - NA_KW // 2, 0, w - NA_KW)
    col_mask = (col[None] >= col_start[:, None]) & (col[None] < col_start[:, None] + NA_KW)
    dc_idx = np.clip(col[None] - col[:, None], -(NA_KW - 1), NA_KW - 1) + NA_KW - 1
    rb = V7X_SUBLANES * 5
    n_rows = n_heads * n_dr
    assert n_rows % rb == 0
    toeplitz = pl.pallas_call(
        _na_bias_kernel,
        grid=(n_rows // rb,),
        in_specs=[
            pl.BlockSpec((rb, n_dc), lambda i: (i, 0)),
            pl.BlockSpec((1, w * w), lambda i: (0, 0)),
            pl.BlockSpec((1, w * w), lambda i: (0, 0)),
        ],
        out_specs=pl.BlockSpec((rb, w * w), lambda i: (i, 0)),
        out_shape=jax.ShapeDtypeStruct((n_rows, w * w), F32),
        compiler_params=_params("arbitrary"),
        name="na_bias_toeplitz",
    )(rpb.reshape(n_rows, n_dc), jnp.asarray(dc_idx.reshape(1, -1), jnp.int32),
      jnp.asarray(col_mask.reshape(1, -1), jnp.int32))
    toeplitz = toeplitz.reshape(n_heads, n_dr, w, w)

    g, u, kh = NA_ROW_GROUP, NA_UNION, min(NA_MAX_KH, rows)
    variants = []
    for rg, u0 in ((0, 0), (g, 0), (rows - g, rows - u)):
        blocks = []
        for gi in range(g):
            r = rg + gi
            start = int(np.clip(r - kh // 2, 0, rows - kh))
            row_blocks = []
            for j in range(u):
                key_row = u0 + j
                if start <= key_row < start + kh:
                    row_blocks.append(toeplitz[:, key_row - r + NA_MAX_KH - 1])
                else:
                    row_blocks.append(jnp.full((n_heads, w, w), NEG_BIG, F32))
            blocks.append(jnp.concatenate(row_blocks, axis=-1))
        variants.append(jnp.concatenate(blocks, axis=-2))
    return jnp.stack(variants, axis=1)


def _na_kernel(q_ref, k_ref, v_ref, qc_ref, kc_ref, vc_ref, bias_ref, o_ref, oc_ref, *, rows):
    w = GRID_W
    gq = NA_ROW_GROUP * w
    gk = NA_UNION * w
    n_groups = rows // NA_ROW_GROUP
    nt = (((1,), (1,)), ((), ()))
    kc = kc_ref[...]
    vc = vc_ref[...]

    def body(gi, carry):
        variant = jnp.where(gi == 0, 0, jnp.where(gi == n_groups - 1, 2, 1))
        u0 = jnp.clip(gi * NA_ROW_GROUP - NA_MAX_KH // 2, 0, rows - NA_UNION)
        q_rows = pl.ds(pl.multiple_of(gi * gq, gq), gq)
        k_rows = pl.ds(pl.multiple_of(u0 * w, w), gk)
        qg = q_ref[q_rows, :]
        s_lat = lax.dot_general(qg, k_ref[k_rows, :], nt, preferred_element_type=F32) + bias_ref[0, variant]
        s_ctx = lax.dot_general(qg, kc, nt, preferred_element_type=F32)
        m = jnp.maximum(jnp.max(s_lat, axis=-1, keepdims=True), jnp.max(s_ctx, axis=-1, keepdims=True))
        p_lat = jnp.exp(s_lat - m)
        p_ctx = jnp.exp(s_ctx - m)
        denom = jnp.sum(p_lat, axis=-1, keepdims=True) + jnp.sum(p_ctx, axis=-1, keepdims=True)
        o = jnp.dot(p_lat.astype(BF16), v_ref[k_rows, :], preferred_element_type=F32)
        o = o + jnp.dot(p_ctx.astype(BF16), vc, preferred_element_type=F32)
        o_ref[q_rows, :] = (o / denom).astype(o_ref.dtype)
        return carry

    lax.fori_loop(0, n_groups, body, 0, unroll=4)

    s_c = lax.dot_general(qc_ref[...], kc, nt, preferred_element_type=F32)
    p_c = jnp.exp(s_c - jnp.max(s_c, axis=-1, keepdims=True))
    o_c = jnp.dot(p_c.astype(BF16), vc, preferred_element_type=F32)
    oc_ref[...] = (o_c / jnp.sum(p_c, axis=-1, keepdims=True)).astype(oc_ref.dtype)


def _na_attention(qkv, bias_tables, n_batch, seq, ctx_len):
    d = qkv.shape[1] // 3
    n_heads = d // NA_HEAD_DIM
    dh = NA_HEAD_DIM
    rows = seq // GRID_W
    assert rows % NA_ROW_GROUP == 0 and rows >= NA_UNION
    lat_blocks = n_batch * seq // ctx_len
    lat = lambda part: pl.BlockSpec((seq, dh), lambda b, h: (b, part * n_heads + h))
    ctx = lambda part: pl.BlockSpec((ctx_len, dh), lambda b, h: (lat_blocks + b, part * n_heads + h))
    o_lat, o_ctx = pl.pallas_call(
        functools.partial(_na_kernel, rows=rows),
        grid=(n_batch, n_heads),
        in_specs=[lat(0), lat(1), lat(2), ctx(0), ctx(1), ctx(2),
                  pl.BlockSpec((1,) + bias_tables.shape[1:], lambda b, h: (h, 0, 0, 0))],
        out_specs=[pl.BlockSpec((seq, dh), lambda b, h: (b, h)),
                   pl.BlockSpec((ctx_len, dh), lambda b, h: (b, h))],
        out_shape=[jax.ShapeDtypeStruct((n_batch * seq, d), BF16),
                   jax.ShapeDtypeStruct((n_batch * ctx_len, d), BF16)],
        compiler_params=_params("arbitrary", "arbitrary"),
        name="na_attention",
    )(qkv, qkv, qkv, qkv, qkv, qkv, bias_tables)
    return o_lat, o_ctx


def _split3(x):
    x1 = x.astype(BF16)
    r1 = x - x1.astype(F32)
    x2 = r1.astype(BF16)
    x3 = (r1 - x2.astype(F32)).astype(BF16)
    return x1, x2, x3


def _gla_prep_kernel(q_ref, k_ref, gl_ref, gup_ref, gbias_ref, cos_ref, sin_ref, tril_ref, triu_ref,
                     qo_ref, ko_ref, bf_ref, bb_ref):
    cos = cos_ref[...]
    sin = sin_ref[...]

    def rope(x):
        halves = [pltpu.roll(x[:, s:s + V7X_LANES], V7X_LANES // 2, 1)
                  for s in range(0, x.shape[1], V7X_LANES)]
        return x * cos + jnp.concatenate(halves, axis=1) * sin

    qo_ref[...] = rope(q_ref[...])
    ko_ref[...] = rope(k_ref[...])

    gl = gl_ref[...].astype(BF16)
    tri_rows = tril_ref.shape[0]
    for direction, (tri_ref, out_ref) in enumerate(((tril_ref, bf_ref), (triu_ref, bb_ref))):
        pre = jnp.dot(gl, gup_ref[direction].astype(BF16), preferred_element_type=F32) + gbias_ref[direction]
        g = (jnp.minimum(pre, 0.0) - jnp.log(1.0 + jnp.exp(-jnp.abs(pre)))) * (LOG2_E / GLA_GATE_NORM)
        tri = tri_ref[...]
        for r0 in range(0, g.shape[0], tri_rows):
            parts = _split3(g[r0:r0 + tri_rows])
            out_ref[r0:r0 + tri_rows, :] = sum(jnp.dot(tri, p, preferred_element_type=F32) for p in parts)


def _gla_prep(qk, gl, gate_up_padded, gate_bias, cos_tab, sin_tab):
    m = qk.shape[0]
    hk = qk.shape[1] // 2
    dk = hk // GLA_HEADS
    tm = ROW_TILE
    tri_rows = 256
    c = GLA_CHUNK
    idx = np.arange(tri_rows)
    same = (idx[:, None] // c) == (idx[None] // c)
    tril = jnp.asarray(same & (idx[None] <= idx[:, None]), BF16)
    triu = jnp.asarray(same & (idx[None] >= idx[:, None]), BF16)
    head = lambda off: pl.BlockSpec((tm, dk), lambda i, h: (i, off + h))
    out_sds = jax.ShapeDtypeStruct((m, hk), F32)
    return pl.pallas_call(
        _gla_prep_kernel,
        grid=(m // tm, GLA_HEADS),
        in_specs=[
            head(0), head(GLA_HEADS),
            pl.BlockSpec((tm, V7X_LANES), lambda i, h: (i, 0)),
            pl.BlockSpec((2, V7X_LANES, dk), lambda i, h: (0, 0, h)),
            pl.BlockSpec((2, 1, dk), lambda i, h: (0, 0, h)),
            pl.BlockSpec((tm, dk), lambda i, h: (i, 0)),
            pl.BlockSpec((tm, dk), lambda i, h: (i, 0)),
            pl.BlockSpec((tri_rows, tri_rows), lambda i, h: (0, 0)),
            pl.BlockSpec((tri_rows, tri_rows), lambda i, h: (0, 0)),
        ],
        out_specs=[head(0)] * 4,
        out_shape=[out_sds] * 4,
        compiler_params=_params("arbitrary", "arbitrary"),
        name="gla_prep",
    )(qk, qk, gl, gate_up_padded, gate_bias, cos_tab, sin_tab, tril, triu)


def _gla_scan_kernel(q_ref, k_ref, b_ref, v_ref, *rest, reverse, finish):
    if finish:
        other_ref, r_ref, gain_ref, o_ref, st_scr = rest
    else:
        o_ref, st_scr = rest
    c = GLA_CHUNK
    sub = GLA_SUB
    dk = q_ref.shape[1]
    nt = (((1,), (1,)), ((), ()))
    tn = (((0,), (0,)), ((), ()))

    @pl.when(pl.program_id(2) == 0)
    def _():
        st_scr[...] = jnp.zeros(st_scr.shape, F32)

    row = lax.broadcasted_iota(jnp.int32, (c, c), 0)
    lane = lax.broadcasted_iota(jnp.int32, (c, c), 1)
    t_idx, s_idx = (lane, row) if reverse else (row, lane)
    level_masks = []
    for half in (c // 2, c // 4, c // 8):
        level_masks.append((row // (2 * half) == lane // (2 * half))
                           & (t_idx % (2 * half) >= half) & (s_idx % (2 * half) < half))
    sub_row = lax.broadcasted_iota(jnp.int32, (sub, 1), 0)
    sub_lane = lax.broadcasted_iota(jnp.int32, (sub, c), 1)

    n_chunks = q_ref.shape[0] // c
    order = range(n_chunks - 1, -1, -1) if reverse else range(n_chunks)
    for ci in order:
        rows = slice(ci * c, (ci + 1) * c)
        q = q_ref[rows, :]
        k = k_ref[rows, :]
        b = b_ref[rows, :]
        v = v_ref[rows, :]
        b_end = b[0:1, :] if reverse else b[c - 1:c, :]

        st = st_scr[...]
        o = lax.dot_general((q * jnp.exp2(b)).astype(BF16), st.astype(BF16), nt, preferred_element_type=F32)
        k_dec = (k * jnp.exp2(b_end - b)).astype(BF16)
        st_scr[...] = st * jnp.exp2(b_end) + lax.dot_general(v, k_dec, tn, preferred_element_type=F32)

        a = jnp.zeros((c, c), F32)
        for half, mask in zip((c // 2, c // 4, c // 8), level_masks):
            anchors = []
            for g0 in range(0, c, 2 * half):
                r = g0 + half if reverse else g0 + half - 1
                anchors.append(jnp.broadcast_to(b[r:r + 1, :], (2 * half, dk)))
            anc = anchors[0] if len(anchors) == 1 else jnp.concatenate(anchors, axis=0)
            e_l = jnp.exp2(-jnp.abs(b - anc))
            a = jnp.where(mask, lax.dot_general((q * e_l).astype(BF16), (k * e_l).astype(BF16), nt,
                                                preferred_element_type=F32), a)

        a_rows = []
        for i in range(c // sub):
            blk = slice(i * sub, (i + 1) * sub)
            q_i = q[blk, :]
            b_i = b[blk, :]
            a_i = a[blk, :]
            for s in range(sub):
                j = i * sub + s
                p = q_i * k[j:j + 1, :] * jnp.exp2(b_i - b[j:j + 1, :])
                col = jnp.sum(p, axis=-1, keepdims=True)
                valid = (sub_row <= s) if reverse else (sub_row >= s)
                a_i = jnp.where((sub_lane == j) & valid, col, a_i)
            a_rows.append(a_i)
        a = jnp.concatenate(a_rows, axis=0)

        o = o + jnp.dot(a.astype(BF16), v, preferred_element_type=F32)
        if finish:
            o = o + other_ref[rows, :]
            y = o * lax.rsqrt(jnp.mean(o * o, axis=-1, keepdims=True) + EPS) * gain_ref[...]
            r = r_ref[rows, :].astype(F32)
            o = y * (r * _sigmoid(r))
        o_ref[rows, :] = o.astype(o_ref.dtype)


def _gla_scan(q, k, bcum, vr, n_batch, seq, ctx_len, reverse, other=None, head_gain=None):
    finish = other is not None
    m = q.shape[0]
    dk = q.shape[1] // GLA_HEADS
    dv = 2 * dk
    seg = GLA_SEG
    assert seq % seg == 0 and ctx_len % seg == 0
    n_lat, n_ctx = seq // seg, ctx_len // seg
    lat_blocks = n_batch * n_lat

    def row_block(b, s):
        if reverse:
            return jnp.where(s < n_ctx, lat_blocks + b * n_ctx + (n_ctx - 1 - s),
                             b * n_lat + (n_lat - 1 - (s - n_ctx)))
        return jnp.where(s < n_ctx, lat_blocks + b * n_ctx + s, b * n_lat + (s - n_ctx))

    qk_spec = pl.BlockSpec((seg, dk), lambda b, h, s: (row_block(b, s), h))
    v_spec = pl.BlockSpec((seg, dv), lambda b, h, s: (row_block(b, s), h))
    in_specs = [qk_spec, qk_spec, qk_spec, v_spec]
    args = [q, k, bcum, vr]
    if finish:
        in_specs += [v_spec,
                     pl.BlockSpec((seg, dv), lambda b, h, s: (row_block(b, s), GLA_HEADS + h)),
                     pl.BlockSpec((1, dv), lambda b, h, s: (0, 0))]
        args += [other, vr, head_gain]
    return pl.pallas_call(
        functools.partial(_gla_scan_kernel, reverse=reverse, finish=finish),
        grid=(n_batch, GLA_HEADS, n_ctx + n_lat),
        in_specs=in_specs,
        out_specs=v_spec,
        out_shape=jax.ShapeDtypeStruct((m, GLA_HEADS * dv), BF16 if finish else F32),
        scratch_shapes=[pltpu.VMEM((dv, dk), F32)],
        compiler_params=_params("arbitrary", "arbitrary", "arbitrary"),
        name="gla_scan_bwd" if reverse else "gla_scan_fwd",
    )(*args)


def _rope_tables(n_batch, seq, ctx_len, dk):
    half = dk // 2
    inv = ROPE_BASE ** (-jnp.arange(0, half, 2, dtype=F32) / half)
    t = jnp.arange(seq)

    def tabs(pos):
        ang = pos.astype(F32)[:, None] * inv[None]
        cos, sin = jnp.cos(ang), jnp.sin(ang)
        return jnp.concatenate([cos, cos], axis=-1), jnp.concatenate([-sin, sin], axis=-1)

    cos_r, sin_r = tabs(t // GRID_W)
    cos_c, sin_c = tabs(t % GRID_W)
    cos = jnp.tile(jnp.concatenate([cos_r, cos_c], axis=-1), (n_batch, 1))
    sin = jnp.tile(jnp.concatenate([sin_r, sin_c], axis=-1), (n_batch, 1))
    n_ctx = n_batch * ctx_len
    cos = jnp.concatenate([cos, jnp.ones((n_ctx, dk), F32)], axis=0)
    sin = jnp.concatenate([sin, jnp.zeros((n_ctx, dk), F32)], axis=0)
    return cos, sin


def kernel(x, c, ctx, c_ctx, ada_down, ada_up, ada_bias, norm_gain, ffn_w_in, ffn_w_out, na_w_qkv, na_w_o, na_rpb, gla_w_in, gla_w_o, gla_gate_down, gla_gate_up, gla_gate_bias, gla_head_gain, final_gain):
    n_batch, seq, d = x.shape
    ctx_len = ctx.shape[1]
    depth = ada_down.shape[0]
    lat_rows = n_batch * seq
    m = lat_rows + n_batch * ctx_len
    geom = dict(lat_rows=lat_rows, seq=seq, n_batch=n_batch)
    assert seq % (2 * ROW_TILE) == 0 and (n_batch * ctx_len) % (2 * ROW_TILE) == 0
    n_groups = n_batch + 1
    g8 = -(-n_groups // V7X_SUBLANES) * V7X_SUBLANES

    xs = jnp.concatenate([x.reshape(lat_rows, d), ctx.reshape(n_batch * ctx_len, d)], axis=0)

    cvec = jnp.concatenate([c, c_ctx[None], jnp.zeros((g8 - n_groups, d), F32)], axis=0)
    mods = _ada_modulation(cvec, ada_down, ada_up, ada_bias)
    mods = mods.reshape(depth, N_MOD, g8, 1, d)

    hk = GLA_HEADS * (d // 2 // GLA_HEADS)
    dk = hk // GLA_HEADS
    ones_d = jnp.ones((1, 2 * d), F32)
    cos_tab, sin_tab = _rope_tables(n_batch, seq, ctx_len, dk)

    ffn_w_in, ffn_w_out = ffn_w_in.astype(BF16), ffn_w_out.astype(BF16)
    na_w_qkv, na_w_o = na_w_qkv.astype(BF16), na_w_o.astype(BF16)
    gla_w_in, gla_w_o = gla_w_in.astype(BF16), gla_w_o.astype(BF16)

    for i in range(depth):
        last = i == depth - 1
        mod = mods[i]
        j = i // N_MIXERS
        n_rows = lat_rows if last else m

        def ffn(xs, which, n_rows):
            base = 0 if which == 0 else 6
            f = ffn_w_in.shape[-1] // 2
            tn = _col_tile(f, 512)
            hid = _pipelined_norm_matmul(xs, norm_gain[i, 2 * which][None], mod[base], mod[base + 1], "swiglu",
                                         [(ffn_w_in, (i, which), 0), (ffn_w_in, (i, which), f // tn)],
                                         f, tn, None, BF16, None, geom)
            return _matmul_residual(hid, ffn_w_out, (i, which), xs, 0.5 * mod[base + 2], 0, n_rows, geom)

        xs = ffn(xs, 0, m)

        gain = norm_gain[i, 1][None]
        if i % N_MIXERS == 0:
            col_scale = jnp.concatenate([jnp.full((1, d), NA_HEAD_DIM ** -0.5, F32), jnp.ones((1, 2 * d), F32)], axis=1)
            qkv = _pipelined_norm_matmul(xs, gain, mod[3], mod[4], "linear", [(na_w_qkv, (j,), 0)],
                                         3 * d, _col_tile(3 * d, 1024), col_scale, BF16, None, geom)
            tables = _na_bias_tables(na_rpb[j], seq // GRID_W)
            o_lat, o_ctx = _na_attention(qkv, tables, n_batch, seq, ctx_len)
            xs = _matmul_residual(o_lat, na_w_o, (j,), xs, mod[5], 0, lat_rows, geom)
            if not last:
                xs = _matmul_residual(o_ctx, na_w_o, (j,), xs, mod[5], lat_rows, m - lat_rows, geom)
        else:
            qk_scale = jnp.concatenate([jnp.full((1, hk), dk ** -0.5, F32), jnp.ones((1, hk), F32)], axis=1)
            rank = GLA_GATE_RANK
            gdown = jnp.concatenate([gla_gate_down[j, 0], gla_gate_down[j, 1],
                                     jnp.zeros((d, V7X_LANES - 2 * rank), F32)], axis=1)
            tn = _col_tile(2 * hk, 1024)
            qk, gl = _pipelined_norm_matmul(xs, gain, mod[3], mod[4], "linear", [(gla_w_in, (j,), 0)],
                                            2 * hk, tn // 2, qk_scale, F32, gdown.astype(BF16), geom)
            vr = _pipelined_norm_matmul(xs, gain, mod[3], mod[4], "linear", [(gla_w_in, (j,), 2 * hk // tn)],
                                        2 * d, tn, ones_d, BF16, None, geom)
            gup = jnp.zeros((2, V7X_LANES, hk), F32)
            gup = gup.at[0, :rank].set(gla_gate_up[j, 0]).at[1, rank:2 * rank].set(gla_gate_up[j, 1])
            q_r, k_r, b_f, b_b = _gla_prep(qk, gl, gup, gla_gate_bias[j][:, None, :], cos_tab, sin_tab)
            o_f = _gla_scan(q_r, k_r, b_f, vr, n_batch, seq, ctx_len, reverse=False)
            mixed = _gla_scan(q_r, k_r, b_b, vr, n_batch, seq, ctx_len, reverse=True,
                              other=o_f, head_gain=gla_head_gain[j][None])
            xs = _matmul_residual(mixed, gla_w_o, (j,), xs, mod[5], 0, n_rows, geom)

        xs = ffn(xs, 1, n_rows)

    return _final_rms(xs, final_gain[None], lat_rows).reshape(n_batch, seq, d)
```

```python
import functools

import jax
import jax.numpy as jnp
import numpy as np
from jax import lax
from jax.experimental import pallas as pl
from jax.experimental.pallas import tpu as pltpu

F32 = jnp.float32
BF16 = jnp.bfloat16

GRID_W = 64
N_MIXERS = 2
N_MOD = 9
NA_HEAD_DIM = 128
NA_MAX_KH = 8
NA_KW = 16
GLA_HEADS = 8
GLA_GATE_RANK = 16
GLA_GATE_NORM = 16.0
ROPE_BASE = 10000.0
EPS = 1e-6

V7X_LANES = 128
V7X_SUBLANES = 8
V7X_VMEM_BYTES = 64 * 1024 * 1024
VMEM_LIMIT_BYTES = 56 * 1024 * 1024

ROW_TILE = 512
NORM_ROWS = 16
NA_ROW_GROUP = 4
NA_UNION = NA_ROW_GROUP + NA_MAX_KH - 1
GLA_CHUNK = 64
GLA_SEG = 256
GLA_SUB = 8
GLA_HEADS_PER_STEP = 2
NEG_BIG = -1e30
LOG2_E = 1.4426950408889634


def _params(*semantics):
    return pltpu.CompilerParams(dimension_semantics=semantics, vmem_limit_bytes=VMEM_LIMIT_BYTES)


def _sigmoid(x):
    return 1.0 / (1.0 + jnp.exp(-x))


def _ada_kernel(cv_ref, down_ref, up_ref, bias_ref, out_ref, t_scr):
    @pl.when(pl.program_id(1) == 0)
    def _():
        s = cv_ref[...]
        s = s * _sigmoid(s)
        t_scr[...] = jnp.dot(s.astype(BF16), down_ref[0].astype(BF16), preferred_element_type=F32)

    out_ref[0] = jnp.dot(t_scr[...].astype(BF16), up_ref[0].astype(BF16),
                         preferred_element_type=F32) + bias_ref[0]


def _ada_modulation(cvec, ada_down, ada_up, ada_bias):
    depth, d, r = ada_down.shape
    g8 = cvec.shape[0]
    bias = ada_bias.reshape(depth * N_MOD, 1, d)
    out = pl.pallas_call(
        _ada_kernel,
        grid=(depth, N_MOD),
        in_specs=[
            pl.BlockSpec((g8, d), lambda l, k: (0, 0)),
            pl.BlockSpec((1, d, r), lambda l, k: (l, 0, 0)),
            pl.BlockSpec((1, r, d), lambda l, k: (l, 0, k)),
            pl.BlockSpec((1, 1, d), lambda l, k: (l * N_MOD + k, 0, 0)),
        ],
        out_specs=pl.BlockSpec((1, g8, d), lambda l, k: (l * N_MOD + k, 0, 0)),
        out_shape=jax.ShapeDtypeStruct((depth * N_MOD, g8, d), F32),
        scratch_shapes=[pltpu.VMEM((g8, r), F32)],
        compiler_params=_params("arbitrary", "arbitrary"),
        name="ada_modulation",
    )(cvec, ada_down, ada_up, bias)
    return out.reshape(depth, N_MOD, g8, d)


def _group_of_tile(i, tm, lat_rows, seq, n_batch):
    return jnp.where(i * tm < lat_rows, (i * tm) // seq, n_batch)


def _col_tile(n, pref):
    t = min(pref, n)
    while n % t:
        t //= 2
    assert t % V7X_LANES == 0 or t == n, (n, pref)
    return t


def _weight_spec(lead, rows, tn, col_block0=0):
    return pl.BlockSpec((None,) * len(lead) + (rows, tn), lambda i, j: tuple(lead) + (0, j + col_block0))


def _pipe_decode(t, ns, nj, n_tiles):
    warm = t < ns
    u = jnp.maximum(t - ns, 0)
    i, j = u // nj, u % nj
    norm_tile = jnp.where(warm, 0, jnp.minimum(i + 1, n_tiles - 1))
    norm_slice = jnp.where(warm, t, jnp.minimum(j, ns - 1))
    return i, j, norm_tile, norm_slice


def _pnm_kernel(x_ref, gain_ref, shift_ref, scale_ref, *rest, mode, ns, nj, n_tiles, has_extra):
    if mode == "swiglu":
        wa_ref, wb_ref, out_ref, h_scr, gm_scr, sh_scr = rest
    elif has_extra:
        w_ref, cs_ref, we_ref, out_ref, oute_ref, h_scr, gm_scr, sh_scr = rest
    else:
        w_ref, cs_ref, out_ref, h_scr, gm_scr, sh_scr = rest
    t = pl.program_id(0)
    i, j, _, s = _pipe_decode(t, ns, nj, n_tiles)
    running = t >= ns
    d = x_ref.shape[1]
    sr = x_ref.shape[0]

    @pl.when(jnp.logical_or(t == 0, jnp.logical_and(running, j == 0)))
    def _():
        gm_scr[...] = jnp.broadcast_to(gain_ref[...] * (1.0 + scale_ref[0]), (V7X_SUBLANES, d))
        sh_scr[...] = jnp.broadcast_to(shift_ref[0], (V7X_SUBLANES, d))

    def normalise_slice(h_dst):
        reps = NORM_ROWS // V7X_SUBLANES
        gm = jnp.concatenate([gm_scr[...]] * reps, axis=0)
        sh = jnp.concatenate([sh_scr[...]] * reps, axis=0)
        for r0 in range(0, sr, NORM_ROWS):
            x = x_ref[r0:r0 + NORM_ROWS, :]
            rs = lax.rsqrt(jnp.mean(x * x, axis=-1, keepdims=True) + EPS)
            rows = pl.ds(pl.multiple_of(s * sr + r0, NORM_ROWS), NORM_ROWS)
            h_dst[rows, :] = (x * rs * gm + sh).astype(BF16)

    def step(parity, with_extra):
        h_first = h_scr[parity]
        if mode == "swiglu":
            a = jnp.dot(h_first, wa_ref[...], preferred_element_type=F32)
        else:
            half = w_ref.shape[1] // 2
            a = jnp.dot(h_first, w_ref[:, :half], preferred_element_type=F32)
        normalise_slice(h_scr.at[1 - parity])
        h_second = h_scr[i % 2]
        if mode == "swiglu":
            b = jnp.dot(h_second, wb_ref[...], preferred_element_type=F32)
            out_ref[...] = (a * _sigmoid(a) * b).astype(out_ref.dtype)
        else:
            b = jnp.dot(h_second, w_ref[:, half:], preferred_element_type=F32)
            out_ref[:, :half] = (a * cs_ref[:, :half]).astype(out_ref.dtype)
            out_ref[:, half:] = (b * cs_ref[:, half:]).astype(out_ref.dtype)
        if with_extra:
            oute_ref[...] = jnp.dot(h_second, we_ref[...], preferred_element_type=F32)

    @pl.when(jnp.logical_not(running))
    def _():
        normalise_slice(h_scr.at[0])

    for parity in (0, 1):
        here = jnp.logical_and(running, i % 2 == parity)
        if has_extra:
            pl.when(jnp.logical_and(here, j == 0))(functools.partial(step, parity, True))
            pl.when(jnp.logical_and(here, j > 0))(functools.partial(step, parity, False))
        else:
            pl.when(here)(functools.partial(step, parity, False))


def _pipelined_norm_matmul(x, gain, shift, scale, mode, weights, n, tn, col_scale, out_dtype, extra, geom):
    m, d = x.shape
    tm = 2 * ROW_TILE
    n_tiles, nj = m // tm, n // tn
    ns = 1
    while 2 * ns <= min(nj, 8):
        ns *= 2
    sr = tm // ns
    assert sr % NORM_ROWS == 0
    dec = functools.partial(_pipe_decode, ns=ns, nj=nj, n_tiles=n_tiles)
    grp = functools.partial(_group_of_tile, tm=tm, **geom)
    mod_spec = pl.BlockSpec((1, 1, d), lambda t: (grp(dec(t)[2]), 0, 0))
    in_specs = [
        pl.BlockSpec((sr, d), lambda t: (dec(t)[2] * ns + dec(t)[3], 0)),
        pl.BlockSpec((1, d), lambda t: (0, 0)),
        mod_spec, mod_spec,
    ]
    args = [x, gain, shift, scale]
    for w, lead, col_block0 in weights:
        in_specs.append(pl.BlockSpec((None,) * len(lead) + (d, tn),
                                     lambda t, lead=lead, c0=col_block0: tuple(lead) + (0, dec(t)[1] + c0)))
        args.append(w)
    out_spec = pl.BlockSpec((tm, tn), lambda t: (dec(t)[0], dec(t)[1]))
    out_shape = jax.ShapeDtypeStruct((m, n), out_dtype)
    if mode == "linear":
        in_specs.append(pl.BlockSpec((1, tn), lambda t: (0, dec(t)[1])))
        args.append(col_scale)
    if extra is not None:
        e = extra.shape[1]
        in_specs.append(pl.BlockSpec((d, e), lambda t: (0, 0)))
        args.append(extra)
        out_spec = [out_spec, pl.BlockSpec((tm, e), lambda t: (dec(t)[0], 0))]
        out_shape = [out_shape, jax.ShapeDtypeStruct((m, e), F32)]
    return pl.pallas_call(
        functools.partial(_pnm_kernel, mode=mode, ns=ns, nj=nj, n_tiles=n_tiles, has_extra=extra is not None),
        grid=(ns + n_tiles * nj,),
        in_specs=in_specs,
        out_specs=out_spec,
        out_shape=out_shape,
        scratch_shapes=[pltpu.VMEM((2, tm, d), BF16),
                        pltpu.VMEM((V7X_SUBLANES, d), F32), pltpu.VMEM((V7X_SUBLANES, d), F32)],
        compiler_params=_params("arbitrary"),
        name="norm_" + mode,
    )(*args)


def _mr_kernel(h_ref, w_ref, x_ref, gate_ref, out_ref):
    acc = jnp.dot(h_ref[...], w_ref[...], preferred_element_type=F32)
    out_ref[...] = x_ref[...] + gate_ref[0] * acc


def _matmul_residual(h, w, lead, x, gate, row0, n_rows, geom):
    k = h.shape[1]
    d = x.shape[1]
    tm = 2 * ROW_TILE
    tn = _col_tile(d, 512)
    assert row0 % tm == 0 and n_rows % tm == 0
    t0 = row0 // tm
    grp = functools.partial(_group_of_tile, tm=tm, **geom)
    return pl.pallas_call(
        _mr_kernel,
        grid=(n_rows // tm, d // tn),
        in_specs=[
            pl.BlockSpec((tm, k), lambda i, j: (i, 0)),
            _weight_spec(lead, k, tn),
            pl.BlockSpec((tm, tn), lambda i, j: (i + t0, j)),
            pl.BlockSpec((1, 1, tn), lambda i, j: (grp(i + t0), 0, j)),
        ],
        out_specs=pl.BlockSpec((tm, tn), lambda i, j: (i + t0, j)),
        out_shape=jax.ShapeDtypeStruct(x.shape, F32),
        input_output_aliases={2: 0},
        compiler_params=_params("arbitrary", "arbitrary"),
        name="matmul_residual",
    )(h, w, x, gate)


def _rms_kernel(x_ref, gain_ref, out_ref):
    x = x_ref[...]
    ms = jnp.mean(x * x, axis=-1, keepdims=True)
    out_ref[...] = x * lax.rsqrt(ms + EPS) * gain_ref[...]


def _final_rms(x, gain, n_rows):
    d = x.shape[1]
    tm = 256
    return pl.pallas_call(
        _rms_kernel,
        grid=(n_rows // tm,),
        in_specs=[pl.BlockSpec((tm, d), lambda i: (i, 0)), pl.BlockSpec((1, d), lambda i: (0, 0))],
        out_specs=pl.BlockSpec((tm, d), lambda i: (i, 0)),
        out_shape=jax.ShapeDtypeStruct((n_rows, d), F32),
        compiler_params=_params("arbitrary"),
        name="final_rms",
    )(x, gain)


def _na_table_kernel(rpb_ref, dc_ref, mask_ref, out_ref, *, rows, n_dr, n_dc):
    w = GRID_W
    head = pl.program_id(0)
    dc = dc_ref[...]
    inside = mask_ref[...] > 0
    hits = [dc == c for c in range(n_dc)]
    outside = jnp.full((w, w), NEG_BIG, F32)
    blocks = []
    for dr in range(n_dr):
        acc = jnp.zeros((w, w), F32)
        for c in range(n_dc):
            acc = jnp.where(hits[c], rpb_ref[(head * n_dr + dr) * n_dc + c], acc)
        blocks.append(jnp.where(inside, acc, NEG_BIG))
    g, u, kh = NA_ROW_GROUP, NA_UNION, min(NA_MAX_KH, rows)
    for vi, (rg, u0) in enumerate(((0, 0), (g, 0), (rows - g, rows - u))):
        for gi in range(g):
            r = rg + gi
            start = min(max(r - kh // 2, 0), rows - kh)
            row = [blocks[u0 + j - r + NA_MAX_KH - 1] if start <= u0 + j < start + kh else outside
                   for j in range(u)]
            out_ref[0, vi, gi * w:(gi + 1) * w, :] = jnp.concatenate(row, axis=-1)


def _na_bias_tables(rpb, rows):
    n_heads, n_dr, n_dc = rpb.shape
    w = GRID_W
    col = np.arange(w)
    col_start = np.clip(col - NA_KW // 2, 0, w - NA_KW)
    col_mask = (col[None] >= col_start[:, None]) & (col[None] < col_start[:, None] + NA_KW)
    dc_idx = np.clip(col[None] - col[:, None], -(NA_KW - 1), NA_KW - 1) + NA_KW - 1
    shape = (n_heads, 3, NA_ROW_GROUP * w, NA_UNION * w)
    return pl.pallas_call(
        functools.partial(_na_table_kernel, rows=rows, n_dr=n_dr, n_dc=n_dc),
        grid=(n_heads,),
        in_specs=[
            pl.BlockSpec(memory_space=pltpu.SMEM),
            pl.BlockSpec((w, w), lambda h: (0, 0)),
            pl.BlockSpec((w, w), lambda h: (0, 0)),
        ],
        out_specs=pl.BlockSpec((1,) + shape[1:], lambda h: (h, 0, 0, 0)),
        out_shape=jax.ShapeDtypeStruct(shape, F32),
        compiler_params=_params("arbitrary"),
        name="na_bias_tables",
    )(rpb.reshape(-1), jnp.asarray(dc_idx, jnp.int32), jnp.asarray(col_mask, jnp.int32))


def _na_kernel(q_ref, k_ref, v_ref, qc_ref, kc_ref, vc_ref, bias_ref, o_ref, oc_ref, *, rows):
    w = GRID_W
    gq = NA_ROW_GROUP * w
    gk = NA_UNION * w
    n_groups = rows // NA_ROW_GROUP
    nt = (((1,), (1,)), ((), ()))
    kc = kc_ref[...]
    vc = vc_ref[...]

    def body(gi, carry):
        variant = jnp.where(gi == 0, 0, jnp.where(gi == n_groups - 1, 2, 1))
        u0 = jnp.clip(gi * NA_ROW_GROUP - NA_MAX_KH // 2, 0, rows - NA_UNION)
        q_rows = pl.ds(pl.multiple_of(gi * gq, gq), gq)
        k_rows = pl.ds(pl.multiple_of(u0 * w, w), gk)
        qg = q_ref[q_rows, :]
        s_lat = lax.dot_general(qg, k_ref[k_rows, :], nt, preferred_element_type=F32) + bias_ref[0, variant]
        s_ctx = lax.dot_general(qg, kc, nt, preferred_element_type=F32)
        m = jnp.maximum(jnp.max(s_lat, axis=-1, keepdims=True), jnp.max(s_ctx, axis=-1, keepdims=True))
        p_lat = jnp.exp(s_lat - m)
        p_ctx = jnp.exp(s_ctx - m)
        denom = jnp.sum(p_lat, axis=-1, keepdims=True) + jnp.sum(p_ctx, axis=-1, keepdims=True)
        o = jnp.dot(p_lat.astype(BF16), v_ref[k_rows, :], preferred_element_type=F32)
        o = o + jnp.dot(p_ctx.astype(BF16), vc, preferred_element_type=F32)
        o_ref[q_rows, :] = (o / denom).astype(o_ref.dtype)
        return carry

    lax.fori_loop(0, n_groups, body, 0, unroll=4)

    s_c = lax.dot_general(qc_ref[...], kc, nt, preferred_element_type=F32)
    p_c = jnp.exp(s_c - jnp.max(s_c, axis=-1, keepdims=True))
    o_c = jnp.dot(p_c.astype(BF16), vc, preferred_element_type=F32)
    oc_ref[...] = (o_c / jnp.sum(p_c, axis=-1, keepdims=True)).astype(oc_ref.dtype)


def _na_attention(qkv, bias_tables, n_batch, seq, ctx_len):
    d = qkv.shape[1] // 3
    n_heads = d // NA_HEAD_DIM
    dh = NA_HEAD_DIM
    rows = seq // GRID_W
    assert rows % NA_ROW_GROUP == 0 and rows >= NA_UNION
    lat_blocks = n_batch * seq // ctx_len
    lat = lambda part: pl.BlockSpec((seq, dh), lambda b, h: (b, part * n_heads + h))
    ctx = lambda part: pl.BlockSpec((ctx_len, dh), lambda b, h: (lat_blocks + b, part * n_heads + h))
    o_lat, o_ctx = pl.pallas_call(
        functools.partial(_na_kernel, rows=rows),
        grid=(n_batch, n_heads),
        in_specs=[lat(0), lat(1), lat(2), ctx(0), ctx(1), ctx(2),
                  pl.BlockSpec((1,) + bias_tables.shape[1:], lambda b, h: (h, 0, 0, 0))],
        out_specs=[pl.BlockSpec((seq, dh), lambda b, h: (b, h)),
                   pl.BlockSpec((ctx_len, dh), lambda b, h: (b, h))],
        out_shape=[jax.ShapeDtypeStruct((n_batch * seq, d), BF16),
                   jax.ShapeDtypeStruct((n_batch * ctx_len, d), BF16)],
        compiler_params=_params("arbitrary", "arbitrary"),
        name="na_attention",
    )(qkv, qkv, qkv, qkv, qkv, qkv, bias_tables)
    return o_lat, o_ctx


def _split3(x):
    x1 = x.astype(BF16)
    r1 = x - x1.astype(F32)
    x2 = r1.astype(BF16)
    x3 = (r1 - x2.astype(F32)).astype(BF16)
    return x1, x2, x3


def _gla_prep_kernel(q_ref, k_ref, gl_ref, gup_ref, gbias_ref, cos_ref, sin_ref, tril_ref, triu_ref,
                     qo_ref, ko_ref, bf_ref, bb_ref):
    cos = cos_ref[...]
    sin = sin_ref[...]

    def rope(x):
        halves = [pltpu.roll(x[:, s:s + V7X_LANES], V7X_LANES // 2, 1)
                  for s in range(0, x.shape[1], V7X_LANES)]
        return x * cos + jnp.concatenate(halves, axis=1) * sin

    qo_ref[...] = rope(q_ref[...])
    ko_ref[...] = rope(k_ref[...])

    gl = gl_ref[...].astype(BF16)
    tri_rows = tril_ref.shape[0]
    for direction, (tri_ref, out_ref) in enumerate(((tril_ref, bf_ref), (triu_ref, bb_ref))):
        pre = jnp.dot(gl, gup_ref[direction].astype(BF16), preferred_element_type=F32) + gbias_ref[direction]
        g = (jnp.minimum(pre, 0.0) - jnp.log(1.0 + jnp.exp(-jnp.abs(pre)))) * (LOG2_E / GLA_GATE_NORM)
        tri = tri_ref[...]
        for r0 in range(0, g.shape[0], tri_rows):
            parts = _split3(g[r0:r0 + tri_rows])
            out_ref[r0:r0 + tri_rows, :] = sum(jnp.dot(tri, p, preferred_element_type=F32) for p in parts)


def _gla_prep(qk, gl, gate_up_padded, gate_bias, cos_tab, sin_tab):
    m = qk.shape[0]
    hk = qk.shape[1] // 2
    dk = hk // GLA_HEADS
    tm = ROW_TILE
    tri_rows = 256
    c = GLA_CHUNK
    idx = np.arange(tri_rows)
    same = (idx[:, None] // c) == (idx[None] // c)
    tril = jnp.asarray(same & (idx[None] <= idx[:, None]), BF16)
    triu = jnp.asarray(same & (idx[None] >= idx[:, None]), BF16)
    head = lambda off: pl.BlockSpec((tm, dk), lambda i, h: (i, off + h))
    out_sds = jax.ShapeDtypeStruct((m, hk), F32)
    return pl.pallas_call(
        _gla_prep_kernel,
        grid=(m // tm, GLA_HEADS),
        in_specs=[
            head(0), head(GLA_HEADS),
            pl.BlockSpec((tm, V7X_LANES), lambda i, h: (i, 0)),
            pl.BlockSpec((2, V7X_LANES, dk), lambda i, h: (0, 0, h)),
            pl.BlockSpec((2, 1, dk), lambda i, h: (0, 0, h)),
            pl.BlockSpec((tm, dk), lambda i, h: (i, 0)),
            pl.BlockSpec((tm, dk), lambda i, h: (i, 0)),
            pl.BlockSpec((tri_rows, tri_rows), lambda i, h: (0, 0)),
            pl.BlockSpec((tri_rows, tri_rows), lambda i, h: (0, 0)),
        ],
        out_specs=[head(0)] * 4,
        out_shape=[out_sds] * 4,
        compiler_params=_params("arbitrary", "arbitrary"),
        name="gla_prep",
    )(qk, qk, gl, gate_up_padded, gate_bias, cos_tab, sin_tab, tril, triu)


def _gla_scan_kernel(q_ref, k_ref, b_ref, v_ref, *rest, reverse, finish, heads):
    if finish:
        other_ref, r_ref, gain_ref, o_ref, st_scr = rest
    else:
        o_ref, st_scr = rest
    c = GLA_CHUNK
    sub = GLA_SUB
    dk = q_ref.shape[1] // heads
    dv = v_ref.shape[1] // heads
    nt = (((1,), (1,)), ((), ()))
    tn = (((0,), (0,)), ((), ()))

    @pl.when(pl.program_id(2) == 0)
    def _():
        st_scr[...] = jnp.zeros(st_scr.shape, F32)

    row = lax.broadcasted_iota(jnp.int32, (c, c), 0)
    lane = lax.broadcasted_iota(jnp.int32, (c, c), 1)
    t_idx, s_idx = (lane, row) if reverse else (row, lane)
    level_masks = []
    for half in (c // 2, c // 4, c // 8):
        level_masks.append((row // (2 * half) == lane // (2 * half))
                           & (t_idx % (2 * half) >= half) & (s_idx % (2 * half) < half))
    sub_row = lax.broadcasted_iota(jnp.int32, (sub, 1), 0)
    sub_lane = lax.broadcasted_iota(jnp.int32, (sub, c), 1)

    n_chunks = q_ref.shape[0] // c
    order = range(n_chunks - 1, -1, -1) if reverse else range(n_chunks)
    def chunk(ci, hh):
        rows = slice(ci * c, (ci + 1) * c)
        kcols = slice(hh * dk, (hh + 1) * dk)
        vcols = slice(hh * dv, (hh + 1) * dv)
        q = q_ref[rows, kcols]
        k = k_ref[rows, kcols]
        b = b_ref[rows, kcols]
        v = v_ref[rows, vcols]
        b_end = b[0:1, :] if reverse else b[c - 1:c, :]

        st = st_scr[hh]
        o = lax.dot_general((q * jnp.exp2(b)).astype(BF16), st.astype(BF16), nt, preferred_element_type=F32)
        k_dec = (k * jnp.exp2(b_end - b)).astype(BF16)
        st_scr[hh] = st * jnp.exp2(b_end) + lax.dot_general(v, k_dec, tn, preferred_element_type=F32)

        a = jnp.zeros((c, c), F32)
        for half, mask in zip((c // 2, c // 4, c // 8), level_masks):
            anchors = []
            for g0 in range(0, c, 2 * half):
                r = g0 + half if reverse else g0 + half - 1
                anchors.append(jnp.broadcast_to(b[r:r + 1, :], (2 * half, dk)))
            anc = anchors[0] if len(anchors) == 1 else jnp.concatenate(anchors, axis=0)
            e_l = jnp.exp2(-jnp.abs(b - anc))
            a = jnp.where(mask, lax.dot_general((q * e_l).astype(BF16), (k * e_l).astype(BF16), nt,
                                                preferred_element_type=F32), a)

        a_rows = []
        for i in range(c // sub):
            blk = slice(i * sub, (i + 1) * sub)
            q_i = q[blk, :]
            b_i = b[blk, :]
            a_i = a[blk, :]
            for s in range(sub):
                j = i * sub + s
                p = q_i * k[j:j + 1, :] * jnp.exp2(b_i - b[j:j + 1, :])
                col = jnp.sum(p, axis=-1, keepdims=True)
                valid = (sub_row <= s) if reverse else (sub_row >= s)
                a_i = jnp.where((sub_lane == j) & valid, col, a_i)
            a_rows.append(a_i)
        a = jnp.concatenate(a_rows, axis=0)

        o = o + jnp.dot(a.astype(BF16), v, preferred_element_type=F32)
        if finish:
            o = o + other_ref[rows, vcols]
            y = o * lax.rsqrt(jnp.mean(o * o, axis=-1, keepdims=True) + EPS) * gain_ref[...]
            r = r_ref[rows, vcols].astype(F32)
            o = y * (r * _sigmoid(r))
        o_ref[rows, vcols] = o.astype(o_ref.dtype)

    for ci in order:
        for hh in range(heads):
            chunk(ci, hh)


def _gla_scan(q, k, bcum, vr, n_batch, seq, ctx_len, reverse, other=None, head_gain=None):
    finish = other is not None
    m = q.shape[0]
    dk = q.shape[1] // GLA_HEADS
    dv = 2 * dk
    seg = GLA_SEG
    assert seq % seg == 0 and ctx_len % seg == 0
    n_lat, n_ctx = seq // seg, ctx_len // seg
    lat_blocks = n_batch * n_lat

    def row_block(b, s):
        if reverse:
            return jnp.where(s < n_ctx, lat_blocks + b * n_ctx + (n_ctx - 1 - s),
                             b * n_lat + (n_lat - 1 - (s - n_ctx)))
        return jnp.where(s < n_ctx, lat_blocks + b * n_ctx + s, b * n_lat + (s - n_ctx))

    hp = GLA_HEADS_PER_STEP
    assert GLA_HEADS % hp == 0
    n_hb = GLA_HEADS // hp
    qk_spec = pl.BlockSpec((seg, hp * dk), lambda b, h, s: (row_block(b, s), h))
    v_spec = pl.BlockSpec((seg, hp * dv), lambda b, h, s: (row_block(b, s), h))
    in_specs = [qk_spec, qk_spec, qk_spec, v_spec]
    args = [q, k, bcum, vr]
    if finish:
        in_specs += [v_spec,
                     pl.BlockSpec((seg, hp * dv), lambda b, h, s: (row_block(b, s), n_hb + h)),
                     pl.BlockSpec((1, dv), lambda b, h, s: (0, 0))]
        args += [other, vr, head_gain]
    return pl.pallas_call(
        functools.partial(_gla_scan_kernel, reverse=reverse, finish=finish, heads=hp),
        grid=(n_batch, n_hb, n_ctx + n_lat),
        in_specs=in_specs,
        out_specs=v_spec,
        out_shape=jax.ShapeDtypeStruct((m, GLA_HEADS * dv), BF16 if finish else F32),
        scratch_shapes=[pltpu.VMEM((hp, dv, dk), F32)],
        compiler_params=_params("arbitrary", "arbitrary", "arbitrary"),
        name="gla_scan_bwd" if reverse else "gla_scan_fwd",
    )(*args)


def _rope_tables(n_batch, seq, ctx_len, dk):
    half = dk // 2
    inv = ROPE_BASE ** (-jnp.arange(0, half, 2, dtype=F32) / half)
    t = jnp.arange(seq)

    def tabs(pos):
        ang = pos.astype(F32)[:, None] * inv[None]
        cos, sin = jnp.cos(ang), jnp.sin(ang)
        return jnp.concatenate([cos, cos], axis=-1), jnp.concatenate([-sin, sin], axis=-1)

    cos_r, sin_r = tabs(t // GRID_W)
    cos_c, sin_c = tabs(t % GRID_W)
    cos = jnp.tile(jnp.concatenate([cos_r, cos_c], axis=-1), (n_batch, 1))
    sin = jnp.tile(jnp.concatenate([sin_r, sin_c], axis=-1), (n_batch, 1))
    n_ctx = n_batch * ctx_len
    cos = jnp.concatenate([cos, jnp.ones((n_ctx, dk), F32)], axis=0)
    sin = jnp.concatenate([sin, jnp.zeros((n_ctx, dk), F32)], axis=0)
    return cos, sin


def kernel(x, c, ctx, c_ctx, ada_down, ada_up, ada_bias, norm_gain, ffn_w_in, ffn_w_out, na_w_qkv, na_w_o, na_rpb, gla_w_in, gla_w_o, gla_gate_down, gla_gate_up, gla_gate_bias, gla_head_gain, final_gain):
    n_batch, seq, d = x.shape
    ctx_len = ctx.shape[1]
    depth = ada_down.shape[0]
    lat_rows = n_batch * seq
    m = lat_rows + n_batch * ctx_len
    geom = dict(lat_rows=lat_rows, seq=seq, n_batch=n_batch)
    assert seq % (2 * ROW_TILE) == 0 and (n_batch * ctx_len) % (2 * ROW_TILE) == 0
    n_groups = n_batch + 1
    g8 = -(-n_groups // V7X_SUBLANES) * V7X_SUBLANES

    xs = jnp.concatenate([x.reshape(lat_rows, d), ctx.reshape(n_batch * ctx_len, d)], axis=0)

    cvec = jnp.concatenate([c, c_ctx[None], jnp.zeros((g8 - n_groups, d), F32)], axis=0)
    mods = _ada_modulation(cvec, ada_down, ada_up, ada_bias)
    mods = mods.reshape(depth, N_MOD, g8, 1, d)

    hk = GLA_HEADS * (d // 2 // GLA_HEADS)
    dk = hk // GLA_HEADS
    ones_d = jnp.ones((1, 2 * d), F32)
    cos_tab, sin_tab = _rope_tables(n_batch, seq, ctx_len, dk)

    ffn_w_in, ffn_w_out = ffn_w_in.astype(BF16), ffn_w_out.astype(BF16)
    na_w_qkv, na_w_o = na_w_qkv.astype(BF16), na_w_o.astype(BF16)
    gla_w_in, gla_w_o = gla_w_in.astype(BF16), gla_w_o.astype(BF16)

    for i in range(depth):
        last = i == depth - 1
        mod = mods[i]
        j = i // N_MIXERS
        n_rows = lat_rows if last else m

        def ffn(xs, which, n_rows):
            base = 0 if which == 0 else 6
            f = ffn_w_in.shape[-1] // 2
            tn = _col_tile(f, 512)
            hid = _pipelined_norm_matmul(xs, norm_gain[i, 2 * which][None], mod[base], mod[base + 1], "swiglu",
                                         [(ffn_w_in, (i, which), 0), (ffn_w_in, (i, which), f // tn)],
                                         f, tn, None, BF16, None, geom)
            return _matmul_residual(hid, ffn_w_out, (i, which), xs, 0.5 * mod[base + 2], 0, n_rows, geom)

        xs = ffn(xs, 0, m)

        gain = norm_gain[i, 1][None]
        if i % N_MIXERS == 0:
            col_scale = jnp.concatenate([jnp.full((1, d), NA_HEAD_DIM ** -0.5, F32), jnp.ones((1, 2 * d), F32)], axis=1)
            qkv = _pipelined_norm_matmul(xs, gain, mod[3], mod[4], "linear", [(na_w_qkv, (j,), 0)],
                                         3 * d, _col_tile(3 * d, 1024), col_scale, BF16, None, geom)
            tables = _na_bias_tables(na_rpb[j], seq // GRID_W)
            o_lat, o_ctx = _na_attention(qkv, tables, n_batch, seq, ctx_len)
            xs = _matmul_residual(o_lat, na_w_o, (j,), xs, mod[5], 0, lat_rows, geom)
            if not last:
                xs = _matmul_residual(o_ctx, na_w_o, (j,), xs, mod[5], lat_rows, m - lat_rows, geom)
        else:
            qk_scale = jnp.concatenate([jnp.full((1, hk), dk ** -0.5, F32), jnp.ones((1, hk), F32)], axis=1)
            rank = GLA_GATE_RANK
            gdown = jnp.concatenate([gla_gate_down[j, 0], gla_gate_down[j, 1],
                                     jnp.zeros((d, V7X_LANES - 2 * rank), F32)], axis=1)
            tn = _col_tile(2 * hk, 1024)
            qk, gl = _pipelined_norm_matmul(xs, gain, mod[3], mod[4], "linear", [(gla_w_in, (j,), 0)],
                                            2 * hk, tn // 2, qk_scale, F32, gdown.astype(BF16), geom)
            vr = _pipelined_norm_matmul(xs, gain, mod[3], mod[4], "linear", [(gla_w_in, (j,), 2 * hk // tn)],
                                        2 * d, tn, ones_d, BF16, None, geom)
            gup = jnp.zeros((2, V7X_LANES, hk), F32)
            gup = gup.at[0, :rank].set(gla_gate_up[j, 0]).at[1, rank:2 * rank].set(gla_gate_up[j, 1])
            q_r, k_r, b_f, b_b = _gla_prep(qk, gl, gup, gla_gate_bias[j][:, None, :], cos_tab, sin_tab)
            o_f = _gla_scan(q_r, k_r, b_f, vr, n_batch, seq, ctx_len, reverse=False)
            mixed = _gla_scan(q_r, k_r, b_b, vr, n_batch, seq, ctx_len, reverse=True,
                              other=o_f, head_gain=gla_head_gain[j][None])
            xs = _matmul_residual(mixed, gla_w_o, (j,), xs, mod[5], 0, n_rows, geom)

        xs = ffn(xs, 1, n_rows)

    return _final_rms(xs, final_gain[None], lat_rows).reshape(n_batch, seq, d)
```

```python
import functools

import jax
import jax.numpy as jnp
import numpy as np
from jax import lax
from jax.experimental import pallas as pl
from jax.experimental.pallas import tpu as pltpu

F32 = jnp.float32
BF16 = jnp.bfloat16

GRID_W = 64
N_MIXERS = 2
N_MOD = 9
NA_HEAD_DIM = 128
NA_MAX_KH = 8
NA_KW = 16
GLA_HEADS = 8
GLA_GATE_RANK = 16
GLA_GATE_NORM = 16.0
ROPE_BASE = 10000.0
EPS = 1e-6

V7X_LANES = 128
V7X_SUBLANES = 8
V7X_VMEM_BYTES = 64 * 1024 * 1024
VMEM_LIMIT_BYTES = 56 * 1024 * 1024

ROW_TILE = 512
NORM_ROWS = 16
NA_ROW_GROUP = 4
NA_UNION = NA_ROW_GROUP + NA_MAX_KH - 1
GLA_CHUNK = 64
GLA_SEG = 256
GLA_SUB = 8
GLA_HEADS_PER_STEP = 2
NEG_BIG = -1e30
LOG2_E = 1.4426950408889634


def _params(*semantics):
    return pltpu.CompilerParams(dimension_semantics=semantics, vmem_limit_bytes=VMEM_LIMIT_BYTES)


def _sigmoid(x):
    return 1.0 / (1.0 + jnp.exp(-x))


def _ada_kernel(cv_ref, down_ref, up_ref, bias_ref, out_ref, t_scr):
    @pl.when(pl.program_id(1) == 0)
    def _():
        s = cv_ref[...]
        s = s * _sigmoid(s)
        t_scr[...] = jnp.dot(s.astype(BF16), down_ref[0].astype(BF16), preferred_element_type=F32)

    out_ref[0] = jnp.dot(t_scr[...].astype(BF16), up_ref[0].astype(BF16),
                         preferred_element_type=F32) + bias_ref[0]


def _ada_modulation(cvec, ada_down, ada_up, ada_bias):
    depth, d, r = ada_down.shape
    g8 = cvec.shape[0]
    bias = ada_bias.reshape(depth * N_MOD, 1, d)
    out = pl.pallas_call(
        _ada_kernel,
        grid=(depth, N_MOD),
        in_specs=[
            pl.BlockSpec((g8, d), lambda l, k: (0, 0)),
            pl.BlockSpec((1, d, r), lambda l, k: (l, 0, 0)),
            pl.BlockSpec((1, r, d), lambda l, k: (l, 0, k)),
            pl.BlockSpec((1, 1, d), lambda l, k: (l * N_MOD + k, 0, 0)),
        ],
        out_specs=pl.BlockSpec((1, g8, d), lambda l, k: (l * N_MOD + k, 0, 0)),
        out_shape=jax.ShapeDtypeStruct((depth * N_MOD, g8, d), F32),
        scratch_shapes=[pltpu.VMEM((g8, r), F32)],
        compiler_params=_params("arbitrary", "arbitrary"),
        name="ada_modulation",
    )(cvec, ada_down, ada_up, bias)
    return out.reshape(depth, N_MOD, g8, d)


def _group_of_tile(i, tm, lat_rows, seq, n_batch):
    return jnp.where(i * tm < lat_rows, (i * tm) // seq, n_batch)


def _col_tile(n, pref):
    t = min(pref, n)
    while n % t:
        t //= 2
    assert t % V7X_LANES == 0 or t == n, (n, pref)
    return t


def _weight_spec(lead, rows, tn, col_block0=0):
    return pl.BlockSpec((None,) * len(lead) + (rows, tn), lambda i, j: tuple(lead) + (0, j + col_block0))


def _pipe_decode(t, ns, nj, n_tiles):
    warm = t < ns
    u = jnp.maximum(t - ns, 0)
    i, j = u // nj, u % nj
    norm_tile = jnp.where(warm, 0, jnp.minimum(i + 1, n_tiles - 1))
    norm_slice = jnp.where(warm, t, jnp.minimum(j, ns - 1))
    return i, j, norm_tile, norm_slice


def _pnm_kernel(x_ref, gain_ref, shift_ref, scale_ref, *rest, mode, ns, nj, n_tiles, has_extra, cast, n_alias):
    n_w = 2 if mode == "swiglu" else 1
    rest = list(rest)
    w_refs = [rest.pop(0) for _ in range(n_w)]
    cs_ref = rest.pop(0) if mode == "linear" else None
    we_ref = rest.pop(0) if has_extra else None
    del rest[:n_alias]
    out_ref = rest.pop(0)
    oute_ref = rest.pop(0) if has_extra else None
    wq_refs = [rest.pop(0) for _ in range(n_w)] if cast else None
    h_scr, gm_scr, sh_scr = rest
    t = pl.program_id(0)
    i, j, _, s = _pipe_decode(t, ns, nj, n_tiles)
    running = t >= ns
    d = x_ref.shape[1]
    sr = x_ref.shape[0]

    @pl.when(jnp.logical_or(t == 0, jnp.logical_and(running, j == 0)))
    def _():
        gm_scr[...] = jnp.broadcast_to(gain_ref[...] * (1.0 + scale_ref[0]), (V7X_SUBLANES, d))
        sh_scr[...] = jnp.broadcast_to(shift_ref[0], (V7X_SUBLANES, d))

    def normalise_slice(h_dst):
        reps = NORM_ROWS // V7X_SUBLANES
        gm = jnp.concatenate([gm_scr[...]] * reps, axis=0)
        sh = jnp.concatenate([sh_scr[...]] * reps, axis=0)
        for r0 in range(0, sr, NORM_ROWS):
            x = x_ref[r0:r0 + NORM_ROWS, :]
            rs = lax.rsqrt(jnp.mean(x * x, axis=-1, keepdims=True) + EPS)
            rows = pl.ds(pl.multiple_of(s * sr + r0, NORM_ROWS), NORM_ROWS)
            h_dst[rows, :] = (x * rs * gm + sh).astype(BF16)

    def weight(k):
        if not cast:
            return w_refs[k]
        wq_refs[k][...] = w_refs[k][...].astype(BF16)
        return wq_refs[k]

    def step(parity, with_extra):
        h_first = h_scr[parity]
        if mode == "swiglu":
            a = jnp.dot(h_first, weight(0)[...], preferred_element_type=F32)
        else:
            w = weight(0)
            half = w.shape[1] // 2
            a = jnp.dot(h_first, w[:, :half], preferred_element_type=F32)
        normalise_slice(h_scr.at[1 - parity])
        h_second = h_scr[i % 2]
        if mode == "swiglu":
            b = jnp.dot(h_second, weight(1)[...], preferred_element_type=F32)
            out_ref[...] = (a * _sigmoid(a) * b).astype(out_ref.dtype)
        else:
            b = jnp.dot(h_second, w[:, half:], preferred_element_type=F32)
            out_ref[:, :half] = (a * cs_ref[:, :half]).astype(out_ref.dtype)
            out_ref[:, half:] = (b * cs_ref[:, half:]).astype(out_ref.dtype)
        if with_extra:
            oute_ref[...] = jnp.dot(h_second, we_ref[...], preferred_element_type=F32)

    @pl.when(jnp.logical_not(running))
    def _():
        normalise_slice(h_scr.at[0])

    for parity in (0, 1):
        here = jnp.logical_and(running, i % 2 == parity)
        if has_extra:
            pl.when(jnp.logical_and(here, j == 0))(functools.partial(step, parity, True))
            pl.when(jnp.logical_and(here, j > 0))(functools.partial(step, parity, False))
        else:
            pl.when(here)(functools.partial(step, parity, False))


def _pnm_call(x, gain, shift, scale, mode, weights, n, tn, col_scale, out_dtype, extra, geom,
              tile0, n_tiles, cast, alias):
    m, d = x.shape
    tm = 2 * ROW_TILE
    nj = n // tn
    ns = 1
    while 2 * ns <= min(nj, 8):
        ns *= 2
    sr = tm // ns
    assert sr % NORM_ROWS == 0
    dec = functools.partial(_pipe_decode, ns=ns, nj=nj, n_tiles=n_tiles)
    grp = functools.partial(_group_of_tile, tm=tm, **geom)
    mod_spec = pl.BlockSpec((1, 1, d), lambda t: (grp(tile0 + dec(t)[2]), 0, 0))
    in_specs = [
        pl.BlockSpec((sr, d), lambda t: ((tile0 + dec(t)[2]) * ns + dec(t)[3], 0)),
        pl.BlockSpec((1, d), lambda t: (0, 0)),
        mod_spec, mod_spec,
    ]
    args = [x, gain, shift, scale]
    for w, lead, col0 in weights:
        assert col0 % tn == 0
        in_specs.append(pl.BlockSpec((None,) * len(lead) + (d, tn),
                                     lambda t, lead=lead, c0=col0 // tn: tuple(lead) + (0, dec(t)[1] + c0)))
        args.append(w)
    if mode == "linear":
        in_specs.append(pl.BlockSpec((1, tn), lambda t: (0, dec(t)[1])))
        args.append(col_scale)
    if extra is not None:
        in_specs.append(pl.BlockSpec(extra.shape, lambda t: (0, 0)))
        args.append(extra)
    aliases = {}
    for k, prev in enumerate(alias):
        aliases[len(args)] = k
        in_specs.append(pl.BlockSpec(memory_space=pl.ANY))
        args.append(prev)
    out_specs = [pl.BlockSpec((tm, tn), lambda t: (tile0 + dec(t)[0], dec(t)[1]))]
    out_shape = [jax.ShapeDtypeStruct((m, n), out_dtype)]
    if extra is not None:
        out_specs.append(pl.BlockSpec((tm, extra.shape[1]), lambda t: (tile0 + dec(t)[0], 0)))
        out_shape.append(jax.ShapeDtypeStruct((m, extra.shape[1]), F32))
    if cast:
        for _ in weights:
            out_specs.append(pl.BlockSpec((d, tn), lambda t: (0, dec(t)[1])))
            out_shape.append(jax.ShapeDtypeStruct((d, n), BF16))
    return pl.pallas_call(
        functools.partial(_pnm_kernel, mode=mode, ns=ns, nj=nj, n_tiles=n_tiles, has_extra=extra is not None,
                          cast=cast, n_alias=len(alias)),
        grid=(ns + n_tiles * nj,),
        in_specs=in_specs,
        out_specs=out_specs,
        out_shape=out_shape,
        input_output_aliases=aliases,
        scratch_shapes=[pltpu.VMEM((2, tm, d), BF16),
                        pltpu.VMEM((V7X_SUBLANES, d), F32), pltpu.VMEM((V7X_SUBLANES, d), F32)],
        compiler_params=_params("arbitrary"),
        name="norm_" + mode + ("_head" if cast else ""),
    )(*args)


def _pipelined_norm_matmul(x, gain, shift, scale, mode, weights, n, tn, col_scale, out_dtype, extra, geom):
    n_tiles = x.shape[0] // (2 * ROW_TILE)
    n_main = 2 if extra is not None else 1
    head = _pnm_call(x, gain, shift, scale, mode, weights, n, tn // 2, col_scale, out_dtype, extra, geom,
                     0, 1, True, [])
    outs, rounded = head[:n_main], head[n_main:]
    if n_tiles > 1:
        outs = _pnm_call(x, gain, shift, scale, mode, [(w, (), 0) for w in rounded], n, tn, col_scale, out_dtype,
                         extra, geom, 1, n_tiles - 1, False, list(outs))
    return outs[0] if extra is None else tuple(outs)


def _mr_kernel(h_ref, w_ref, x_ref, gate_ref, out_ref):
    acc = jnp.dot(h_ref[...], w_ref[...], preferred_element_type=F32)
    out_ref[...] = x_ref[...] + gate_ref[0] * acc


def _matmul_residual(h, w, lead, x, gate, row0, n_rows, geom):
    k = h.shape[1]
    d = x.shape[1]
    tm = 2 * ROW_TILE
    tn = _col_tile(d, 512)
    assert row0 % tm == 0 and n_rows % tm == 0
    t0 = row0 // tm
    grp = functools.partial(_group_of_tile, tm=tm, **geom)
    return pl.pallas_call(
        _mr_kernel,
        grid=(n_rows // tm, d // tn),
        in_specs=[
            pl.BlockSpec((tm, k), lambda i, j: (i, 0)),
            _weight_spec(lead, k, tn),
            pl.BlockSpec((tm, tn), lambda i, j: (i + t0, j)),
            pl.BlockSpec((1, 1, tn), lambda i, j: (grp(i + t0), 0, j)),
        ],
        out_specs=pl.BlockSpec((tm, tn), lambda i, j: (i + t0, j)),
        out_shape=jax.ShapeDtypeStruct(x.shape, F32),
        input_output_aliases={2: 0},
        compiler_params=_params("arbitrary", "arbitrary"),
        name="matmul_residual",
    )(h, w, x, gate)


def _rms_kernel(x_ref, gain_ref, out_ref):
    x = x_ref[...]
    ms = jnp.mean(x * x, axis=-1, keepdims=True)
    out_ref[...] = x * lax.rsqrt(ms + EPS) * gain_ref[...]


def _final_rms(x, gain, n_rows):
    d = x.shape[1]
    tm = 256
    return pl.pallas_call(
        _rms_kernel,
        grid=(n_rows // tm,),
        in_specs=[pl.BlockSpec((tm, d), lambda i: (i, 0)), pl.BlockSpec((1, d), lambda i: (0, 0))],
        out_specs=pl.BlockSpec((tm, d), lambda i: (i, 0)),
        out_shape=jax.ShapeDtypeStruct((n_rows, d), F32),
        compiler_params=_params("arbitrary"),
        name="final_rms",
    )(x, gain)


def _na_table_kernel(rpb_ref, dc_ref, mask_ref, out_ref, *, rows, n_dr, n_dc):
    w = GRID_W
    head = pl.program_id(0)
    dc = dc_ref[...]
    inside = mask_ref[...] > 0
    hits = [dc == c for c in range(n_dc)]
    outside = jnp.full((w, w), NEG_BIG, F32)
    blocks = []
    for dr in range(n_dr):
        acc = jnp.zeros((w, w), F32)
        for c in range(n_dc):
            acc = jnp.where(hits[c], rpb_ref[(head * n_dr + dr) * n_dc + c], acc)
        blocks.append(jnp.where(inside, acc, NEG_BIG))
    g, u, kh = NA_ROW_GROUP, NA_UNION, min(NA_MAX_KH, rows)
    for vi, (rg, u0) in enumerate(((0, 0), (g, 0), (rows - g, rows - u))):
        for gi in range(g):
            r = rg + gi
            start = min(max(r - kh // 2, 0), rows - kh)
            row = [blocks[u0 + j - r + NA_MAX_KH - 1] if start <= u0 + j < start + kh else outside
                   for j in range(u)]
            out_ref[0, vi, gi * w:(gi + 1) * w, :] = jnp.concatenate(row, axis=-1)


def _na_bias_tables(rpb, rows):
    n_heads, n_dr, n_dc = rpb.shape
    w = GRID_W
    col = np.arange(w)
    col_start = np.clip(col - NA_KW // 2, 0, w - NA_KW)
    col_mask = (col[None] >= col_start[:, None]) & (col[None] < col_start[:, None] + NA_KW)
    dc_idx = np.clip(col[None] - col[:, None], -(NA_KW - 1), NA_KW - 1) + NA_KW - 1
    shape = (n_heads, 3, NA_ROW_GROUP * w, NA_UNION * w)
    return pl.pallas_call(
        functools.partial(_na_table_kernel, rows=rows, n_dr=n_dr, n_dc=n_dc),
        grid=(n_heads,),
        in_specs=[
            pl.BlockSpec(memory_space=pltpu.SMEM),
            pl.BlockSpec((w, w), lambda h: (0, 0)),
            pl.BlockSpec((w, w), lambda h: (0, 0)),
        ],
        out_specs=pl.BlockSpec((1,) + shape[1:], lambda h: (h, 0, 0, 0)),
        out_shape=jax.ShapeDtypeStruct(shape, F32),
        compiler_params=_params("arbitrary"),
        name="na_bias_tables",
    )(rpb.reshape(-1), jnp.asarray(dc_idx, jnp.int32), jnp.asarray(col_mask, jnp.int32))


def _na_kernel(q_ref, k_ref, v_ref, qc_ref, kc_ref, vc_ref, bias_ref, o_ref, oc_ref, *, rows):
    w = GRID_W
    gq = NA_ROW_GROUP * w
    gk = NA_UNION * w
    n_groups = rows // NA_ROW_GROUP
    nt = (((1,), (1,)), ((), ()))
    kc = kc_ref[...]
    vc = vc_ref[...]

    def body(gi, carry):
        variant = jnp.where(gi == 0, 0, jnp.where(gi == n_groups - 1, 2, 1))
        u0 = jnp.clip(gi * NA_ROW_GROUP - NA_MAX_KH // 2, 0, rows - NA_UNION)
        q_rows = pl.ds(pl.multiple_of(gi * gq, gq), gq)
        k_rows = pl.ds(pl.multiple_of(u0 * w, w), gk)
        qg = q_ref[q_rows, :]
        s_lat = lax.dot_general(qg, k_ref[k_rows, :], nt, preferred_element_type=F32) + bias_ref[0, variant]
        s_ctx = lax.dot_general(qg, kc, nt, preferred_element_type=F32)
        m = jnp.maximum(jnp.max(s_lat, axis=-1, keepdims=True), jnp.max(s_ctx, axis=-1, keepdims=True))
        p_lat = jnp.exp(s_lat - m)
        p_ctx = jnp.exp(s_ctx - m)
        denom = jnp.sum(p_lat, axis=-1, keepdims=True) + jnp.sum(p_ctx, axis=-1, keepdims=True)
        o = jnp.dot(p_lat.astype(BF16), v_ref[k_rows, :], preferred_element_type=F32)
        o = o + jnp.dot(p_ctx.astype(BF16), vc, preferred_element_type=F32)
        o_ref[q_rows, :] = (o / denom).astype(o_ref.dtype)
        return carry

    lax.fori_loop(0, n_groups, body, 0, unroll=4)

    s_c = lax.dot_general(qc_ref[...], kc, nt, preferred_element_type=F32)
    p_c = jnp.exp(s_c - jnp.max(s_c, axis=-1, keepdims=True))
    o_c = jnp.dot(p_c.astype(BF16), vc, preferred_element_type=F32)
    oc_ref[...] = (o_c / jnp.sum(p_c, axis=-1, keepdims=True)).astype(oc_ref.dtype)


def _na_attention(qkv, bias_tables, n_batch, seq, ctx_len):
    d = qkv.shape[1] // 3
    n_heads = d // NA_HEAD_DIM
    dh = NA_HEAD_DIM
    rows = seq // GRID_W
    assert rows % NA_ROW_GROUP == 0 and rows >= NA_UNION
    lat_blocks = n_batch * seq // ctx_len
    lat = lambda part: pl.BlockSpec((seq, dh), lambda b, h: (b, part * n_heads + h))
    ctx = lambda part: pl.BlockSpec((ctx_len, dh), lambda b, h: (lat_blocks + b, part * n_heads + h))
    o_lat, o_ctx = pl.pallas_call(
        functools.partial(_na_kernel, rows=rows),
        grid=(n_batch, n_heads),
        in_specs=[lat(0), lat(1), lat(2), ctx(0), ctx(1), ctx(2),
                  pl.BlockSpec((1,) + bias_tables.shape[1:], lambda b, h: (h, 0, 0, 0))],
        out_specs=[pl.BlockSpec((seq, dh), lambda b, h: (b, h)),
                   pl.BlockSpec((ctx_len, dh), lambda b, h: (b, h))],
        out_shape=[jax.ShapeDtypeStruct((n_batch * seq, d), BF16),
                   jax.ShapeDtypeStruct((n_batch * ctx_len, d), BF16)],
        compiler_params=_params("arbitrary", "arbitrary"),
        name="na_attention",
    )(qkv, qkv, qkv, qkv, qkv, qkv, bias_tables)
    return o_lat, o_ctx


def _split3(x):
    x1 = x.astype(BF16)
    r1 = x - x1.astype(F32)
    x2 = r1.astype(BF16)
    x3 = (r1 - x2.astype(F32)).astype(BF16)
    return x1, x2, x3


def _gla_prep_kernel(q_ref, k_ref, gl_ref, gup_ref, gbias_ref, cos_ref, sin_ref, tril_ref, triu_ref,
                     qo_ref, ko_ref, bf_ref, bb_ref):
    cos = cos_ref[...]
    sin = sin_ref[...]

    def rope(x):
        halves = [pltpu.roll(x[:, s:s + V7X_LANES], V7X_LANES // 2, 1)
                  for s in range(0, x.shape[1], V7X_LANES)]
        return x * cos + jnp.concatenate(halves, axis=1) * sin

    qo_ref[...] = rope(q_ref[...])
    ko_ref[...] = rope(k_ref[...])

    gl = gl_ref[...].astype(BF16)
    tri_rows = tril_ref.shape[0]
    for direction, (tri_ref, out_ref) in enumerate(((tril_ref, bf_ref), (triu_ref, bb_ref))):
        pre = jnp.dot(gl, gup_ref[direction].astype(BF16), preferred_element_type=F32) + gbias_ref[direction]
        g = (jnp.minimum(pre, 0.0) - jnp.log(1.0 + jnp.exp(-jnp.abs(pre)))) * (LOG2_E / GLA_GATE_NORM)
        tri = tri_ref[...]
        for r0 in range(0, g.shape[0], tri_rows):
            parts = _split3(g[r0:r0 + tri_rows])
            out_ref[r0:r0 + tri_rows, :] = sum(jnp.dot(tri, p, preferred_element_type=F32) for p in parts)


def _gla_prep(qk, gl, gate_up_padded, gate_bias, cos_tab, sin_tab):
    m = qk.shape[0]
    hk = qk.shape[1] // 2
    dk = hk // GLA_HEADS
    tm = ROW_TILE
    tri_rows = 256
    c = GLA_CHUNK
    idx = np.arange(tri_rows)
    same = (idx[:, None] // c) == (idx[None] // c)
    tril = jnp.asarray(same & (idx[None] <= idx[:, None]), BF16)
    triu = jnp.asarray(same & (idx[None] >= idx[:, None]), BF16)
    head = lambda off: pl.BlockSpec((tm, dk), lambda i, h: (i, off + h))
    out_sds = jax.ShapeDtypeStruct((m, hk), F32)
    return pl.pallas_call(
        _gla_prep_kernel,
        grid=(m // tm, GLA_HEADS),
        in_specs=[
            head(0), head(GLA_HEADS),
            pl.BlockSpec((tm, V7X_LANES), lambda i, h: (i, 0)),
            pl.BlockSpec((2, V7X_LANES, dk), lambda i, h: (0, 0, h)),
            pl.BlockSpec((2, 1, dk), lambda i, h: (0, 0, h)),
            pl.BlockSpec((tm, dk), lambda i, h: (i, 0)),
            pl.BlockSpec((tm, dk), lambda i, h: (i, 0)),
            pl.BlockSpec((tri_rows, tri_rows), lambda i, h: (0, 0)),
            pl.BlockSpec((tri_rows, tri_rows), lambda i, h: (0, 0)),
        ],
        out_specs=[head(0)] * 4,
        out_shape=[out_sds] * 4,
        compiler_params=_params("arbitrary", "arbitrary"),
        name="gla_prep",
    )(qk, qk, gl, gate_up_padded, gate_bias, cos_tab, sin_tab, tril, triu)


def _gla_scan_kernel(q_ref, k_ref, b_ref, v_ref, *rest, reverse, finish, heads):
    if finish:
        other_ref, r_ref, gain_ref, o_ref, st_scr = rest
    else:
        o_ref, st_scr = rest
    c = GLA_CHUNK
    sub = GLA_SUB
    dk = q_ref.shape[1] // heads
    dv = v_ref.shape[1] // heads
    nt = (((1,), (1,)), ((), ()))
    tn = (((0,), (0,)), ((), ()))

    @pl.when(pl.program_id(2) == 0)
    def _():
        st_scr[...] = jnp.zeros(st_scr.shape, F32)

    row = lax.broadcasted_iota(jnp.int32, (c, c), 0)
    lane = lax.broadcasted_iota(jnp.int32, (c, c), 1)
    t_idx, s_idx = (lane, row) if reverse else (row, lane)
    level_masks = []
    for half in (c // 2, c // 4, c // 8):
        level_masks.append((row // (2 * half) == lane // (2 * half))
                           & (t_idx % (2 * half) >= half) & (s_idx % (2 * half) < half))
    sub_row = lax.broadcasted_iota(jnp.int32, (sub, 1), 0)
    sub_lane = lax.broadcasted_iota(jnp.int32, (sub, c), 1)

    n_chunks = q_ref.shape[0] // c
    order = range(n_chunks - 1, -1, -1) if reverse else range(n_chunks)
    def chunk(ci, hh):
        rows = slice(ci * c, (ci + 1) * c)
        kcols = slice(hh * dk, (hh + 1) * dk)
        vcols = slice(hh * dv, (hh + 1) * dv)
        q = q_ref[rows, kcols]
        k = k_ref[rows, kcols]
        b = b_ref[rows, kcols]
        v = v_ref[rows, vcols]
        b_end = b[0:1, :] if reverse else b[c - 1:c, :]

        st = st_scr[hh]
        o = lax.dot_general((q * jnp.exp2(b)).astype(BF16), st.astype(BF16), nt, preferred_element_type=F32)
        k_dec = (k * jnp.exp2(b_end - b)).astype(BF16)
        st_scr[hh] = st * jnp.exp2(b_end) + lax.dot_general(v, k_dec, tn, preferred_element_type=F32)

        a = jnp.zeros((c, c), F32)
        for half, mask in zip((c // 2, c // 4, c // 8), level_masks):
            anchors = []
            for g0 in range(0, c, 2 * half):
                r = g0 + half if reverse else g0 + half - 1
                anchors.append(jnp.broadcast_to(b[r:r + 1, :], (2 * half, dk)))
            anc = anchors[0] if len(anchors) == 1 else jnp.concatenate(anchors, axis=0)
            e_l = jnp.exp2(-jnp.abs(b - anc))
            a = jnp.where(mask, lax.dot_general((q * e_l).astype(BF16), (k * e_l).astype(BF16), nt,
                                                preferred_element_type=F32), a)

        a_rows = []
        for i in range(c // sub):
            blk = slice(i * sub, (i + 1) * sub)
            q_i = q[blk, :]
            b_i = b[blk, :]
            a_i = a[blk, :]
            for s in range(sub):
                j = i * sub + s
                p = q_i * k[j:j + 1, :] * jnp.exp2(b_i - b[j:j + 1, :])
                col = jnp.sum(p, axis=-1, keepdims=True)
                valid = (sub_row <= s) if reverse else (sub_row >= s)
                a_i = jnp.where((sub_lane == j) & valid, col, a_i)
            a_rows.append(a_i)
        a = jnp.concatenate(a_rows, axis=0)

        o = o + jnp.dot(a.astype(BF16), v, preferred_element_type=F32)
        if finish:
            o = o + other_ref[rows, vcols]
            y = o * lax.rsqrt(jnp.mean(o * o, axis=-1, keepdims=True) + EPS) * gain_ref[...]
            r = r_ref[rows, vcols].astype(F32)
            o = y * (r * _sigmoid(r))
        o_ref[rows, vcols] = o.astype(o_ref.dtype)

    for ci in order:
        for hh in range(heads):
            chunk(ci, hh)


def _gla_scan(q, k, bcum, vr, n_batch, seq, ctx_len, reverse, other=None, head_gain=None):
    finish = other is not None
    m = q.shape[0]
    dk = q.shape[1] // GLA_HEADS
    dv = 2 * dk
    seg = GLA_SEG
    assert seq % seg == 0 and ctx_len % seg == 0
    n_lat, n_ctx = seq // seg, ctx_len // seg
    lat_blocks = n_batch * n_lat

    def row_block(b, s):
        if reverse:
            return jnp.where(s < n_ctx, lat_blocks + b * n_ctx + (n_ctx - 1 - s),
                             b * n_lat + (n_lat - 1 - (s - n_ctx)))
        return jnp.where(s < n_ctx, lat_blocks + b * n_ctx + s, b * n_lat + (s - n_ctx))

    hp = GLA_HEADS_PER_STEP
    assert GLA_HEADS % hp == 0
    n_hb = GLA_HEADS // hp
    qk_spec = pl.BlockSpec((seg, hp * dk), lambda b, h, s: (row_block(b, s), h))
    v_spec = pl.BlockSpec((seg, hp * dv), lambda b, h, s: (row_block(b, s), h))
    in_specs = [qk_spec, qk_spec, qk_spec, v_spec]
    args = [q, k, bcum, vr]
    if finish:
        in_specs += [v_spec,
                     pl.BlockSpec((seg, hp * dv), lambda b, h, s: (row_block(b, s), n_hb + h)),
                     pl.BlockSpec((1, dv), lambda b, h, s: (0, 0))]
        args += [other, vr, head_gain]
    return pl.pallas_call(
        functools.partial(_gla_scan_kernel, reverse=reverse, finish=finish, heads=hp),
        grid=(n_batch, n_hb, n_ctx + n_lat),
        in_specs=in_specs,
        out_specs=v_spec,
        out_shape=jax.ShapeDtypeStruct((m, GLA_HEADS * dv), BF16 if finish else F32),
        scratch_shapes=[pltpu.VMEM((hp, dv, dk), F32)],
        compiler_params=_params("arbitrary", "arbitrary", "arbitrary"),
        name="gla_scan_bwd" if reverse else "gla_scan_fwd",
    )(*args)


def _rope_tables(n_batch, seq, ctx_len, dk):
    half = dk // 2
    inv = ROPE_BASE ** (-jnp.arange(0, half, 2, dtype=F32) / half)
    t = jnp.arange(seq)

    def tabs(pos):
        ang = pos.astype(F32)[:, None] * inv[None]
        cos, sin = jnp.cos(ang), jnp.sin(ang)
        return jnp.concatenate([cos, cos], axis=-1), jnp.concatenate([-sin, sin], axis=-1)

    cos_r, sin_r = tabs(t // GRID_W)
    cos_c, sin_c = tabs(t % GRID_W)
    cos = jnp.tile(jnp.concatenate([cos_r, cos_c], axis=-1), (n_batch, 1))
    sin = jnp.tile(jnp.concatenate([sin_r, sin_c], axis=-1), (n_batch, 1))
    n_ctx = n_batch * ctx_len
    cos = jnp.concatenate([cos, jnp.ones((n_ctx, dk), F32)], axis=0)
    sin = jnp.concatenate([sin, jnp.zeros((n_ctx, dk), F32)], axis=0)
    return cos, sin


def kernel(x, c, ctx, c_ctx, ada_down, ada_up, ada_bias, norm_gain, ffn_w_in, ffn_w_out, na_w_qkv, na_w_o, na_rpb, gla_w_in, gla_w_o, gla_gate_down, gla_gate_up, gla_gate_bias, gla_head_gain, final_gain):
    n_batch, seq, d = x.shape
    ctx_len = ctx.shape[1]
    depth = ada_down.shape[0]
    lat_rows = n_batch * seq
    m = lat_rows + n_batch * ctx_len
    geom = dict(lat_rows=lat_rows, seq=seq, n_batch=n_batch)
    assert seq % (2 * ROW_TILE) == 0 and (n_batch * ctx_len) % (2 * ROW_TILE) == 0
    n_groups = n_batch + 1
    g8 = -(-n_groups // V7X_SUBLANES) * V7X_SUBLANES

    xs = jnp.concatenate([x.reshape(lat_rows, d), ctx.reshape(n_batch * ctx_len, d)], axis=0)

    cvec = jnp.concatenate([c, c_ctx[None], jnp.zeros((g8 - n_groups, d), F32)], axis=0)
    mods = _ada_modulation(cvec, ada_down, ada_up, ada_bias)
    mods = mods.reshape(depth, N_MOD, g8, 1, d)

    hk = GLA_HEADS * (d // 2 // GLA_HEADS)
    dk = hk // GLA_HEADS
    ones_d = jnp.ones((1, 2 * d), F32)
    cos_tab, sin_tab = _rope_tables(n_batch, seq, ctx_len, dk)

    ffn_w_out, na_w_o, gla_w_o = ffn_w_out.astype(BF16), na_w_o.astype(BF16), gla_w_o.astype(BF16)

    for i in range(depth):
        last = i == depth - 1
        mod = mods[i]
        j = i // N_MIXERS
        n_rows = lat_rows if last else m

        def ffn(xs, which, n_rows):
            base = 0 if which == 0 else 6
            f = ffn_w_in.shape[-1] // 2
            tn = _col_tile(f, 512)
            hid = _pipelined_norm_matmul(xs, norm_gain[i, 2 * which][None], mod[base], mod[base + 1], "swiglu",
                                         [(ffn_w_in, (i, which), 0), (ffn_w_in, (i, which), f)],
                                         f, tn, None, BF16, None, geom)
            return _matmul_residual(hid, ffn_w_out, (i, which), xs, 0.5 * mod[base + 2], 0, n_rows, geom)

        xs = ffn(xs, 0, m)

        gain = norm_gain[i, 1][None]
        if i % N_MIXERS == 0:
            col_scale = jnp.concatenate([jnp.full((1, d), NA_HEAD_DIM ** -0.5, F32), jnp.ones((1, 2 * d), F32)], axis=1)
            qkv = _pipelined_norm_matmul(xs, gain, mod[3], mod[4], "linear", [(na_w_qkv, (j,), 0)],
                                         3 * d, _col_tile(3 * d, 1024), col_scale, BF16, None, geom)
            tables = _na_bias_tables(na_rpb[j], seq // GRID_W)
            o_lat, o_ctx = _na_attention(qkv, tables, n_batch, seq, ctx_len)
            xs = _matmul_residual(o_lat, na_w_o, (j,), xs, mod[5], 0, lat_rows, geom)
            if not last:
                xs = _matmul_residual(o_ctx, na_w_o, (j,), xs, mod[5], lat_rows, m - lat_rows, geom)
        else:
            qk_scale = jnp.concatenate([jnp.full((1, hk), dk ** -0.5, F32), jnp.ones((1, hk), F32)], axis=1)
            rank = GLA_GATE_RANK
            gdown = jnp.concatenate([gla_gate_down[j, 0], gla_gate_down[j, 1],
                                     jnp.zeros((d, V7X_LANES - 2 * rank), F32)], axis=1)
            tn = _col_tile(2 * hk, 1024)
            qk, gl = _pipelined_norm_matmul(xs, gain, mod[3], mod[4], "linear", [(gla_w_in, (j,), 0)],
                                            2 * hk, tn // 2, qk_scale, F32, gdown.astype(BF16), geom)
            vr = _pipelined_norm_matmul(xs, gain, mod[3], mod[4], "linear", [(gla_w_in, (j,), 2 * hk)],
                                        2 * d, tn, ones_d, BF16, None, geom)
            gup = jnp.zeros((2, V7X_LANES, hk), F32)
            gup = gup.at[0, :rank].set(gla_gate_up[j, 0]).at[1, rank:2 * rank].set(gla_gate_up[j, 1])
            q_r, k_r, b_f, b_b = _gla_prep(qk, gl, gup, gla_gate_bias[j][:, None, :], cos_tab, sin_tab)
            o_f = _gla_scan(q_r, k_r, b_f, vr, n_batch, seq, ctx_len, reverse=False)
            mixed = _gla_scan(q_r, k_r, b_b, vr, n_batch, seq, ctx_len, reverse=True,
                              other=o_f, head_gain=gla_head_gain[j][None])
            xs = _matmul_residual(mixed, gla_w_o, (j,), xs, mod[5], 0, n_rows, geom)

        xs = ffn(xs, 1, n_rows)

    return _final_rms(xs, final_gain[None], lat_rows).reshape(n_batch, seq, d)
```

```python
import functools

import jax
import jax.numpy as jnp
import numpy as np
from jax import lax
from jax.experimental import pallas as pl
from jax.experimental.pallas import tpu as pltpu

F32 = jnp.float32
BF16 = jnp.bfloat16

GRID_W = 64
N_MIXERS = 2
N_MOD = 9
NA_HEAD_DIM = 128
NA_MAX_KH = 8
NA_KW = 16
GLA_HEADS = 8
GLA_GATE_RANK = 16
GLA_GATE_NORM = 16.0
ROPE_BASE = 10000.0
EPS = 1e-6

V7X_LANES = 128
V7X_SUBLANES = 8
V7X_VMEM_BYTES = 64 * 1024 * 1024
VMEM_LIMIT_BYTES = 56 * 1024 * 1024

ROW_TILE = 512
NORM_ROWS = 16
NA_ROW_GROUP = 4
NA_UNION = NA_ROW_GROUP + NA_MAX_KH - 1
GLA_CHUNK = 64
GLA_SEG = 256
GLA_SUB = 8
GLA_HEADS_PER_STEP = 4
NEG_BIG = -1e30
LOG2_E = 1.4426950408889634


def _params(*semantics):
    return pltpu.CompilerParams(dimension_semantics=semantics, vmem_limit_bytes=VMEM_LIMIT_BYTES)


def _sigmoid(x):
    return 1.0 / (1.0 + jnp.exp(-x))


def _ada_kernel(cv_ref, down_ref, up_ref, bias_ref, out_ref, t_scr):
    @pl.when(pl.program_id(1) == 0)
    def _():
        s = cv_ref[...]
        s = s * _sigmoid(s)
        t_scr[...] = jnp.dot(s.astype(BF16), down_ref[0].astype(BF16), preferred_element_type=F32)

    out_ref[0] = jnp.dot(t_scr[...].astype(BF16), up_ref[0].astype(BF16),
                         preferred_element_type=F32) + bias_ref[0]


def _ada_modulation(cvec, ada_down, ada_up, ada_bias):
    depth, d, r = ada_down.shape
    g8 = cvec.shape[0]
    bias = ada_bias.reshape(depth * N_MOD, 1, d)
    out = pl.pallas_call(
        _ada_kernel,
        grid=(depth, N_MOD),
        in_specs=[
            pl.BlockSpec((g8, d), lambda l, k: (0, 0)),
            pl.BlockSpec((1, d, r), lambda l, k: (l, 0, 0)),
            pl.BlockSpec((1, r, d), lambda l, k: (l, 0, k)),
            pl.BlockSpec((1, 1, d), lambda l, k: (l * N_MOD + k, 0, 0)),
        ],
        out_specs=pl.BlockSpec((1, g8, d), lambda l, k: (l * N_MOD + k, 0, 0)),
        out_shape=jax.ShapeDtypeStruct((depth * N_MOD, g8, d), F32),
        scratch_shapes=[pltpu.VMEM((g8, r), F32)],
        compiler_params=_params("arbitrary", "arbitrary"),
        name="ada_modulation",
    )(cvec, ada_down, ada_up, bias)
    return out.reshape(depth, N_MOD, g8, d)


def _group_of_tile(i, tm, lat_rows, seq, n_batch):
    return jnp.where(i * tm < lat_rows, (i * tm) // seq, n_batch)


def _col_tile(n, pref):
    t = min(pref, n)
    while n % t:
        t //= 2
    assert t % V7X_LANES == 0 or t == n, (n, pref)
    return t


def _weight_spec(lead, rows, tn, col_block0=0):
    return pl.BlockSpec((None,) * len(lead) + (rows, tn), lambda i, j: tuple(lead) + (0, j + col_block0))


def _pipe_decode(t, nw, ns, nj, n_tiles):
    warm = t < nw
    u = jnp.maximum(t - nw, 0)
    i, j = u // nj, u % nj
    norm_tile = jnp.where(warm, 0, jnp.minimum(i + 1, n_tiles - 1))
    norm_slice = jnp.minimum(jnp.where(warm, t, j), ns - 1)
    return i, j, norm_tile, norm_slice


def _pnm_kernel(x_ref, gain_ref, shift_ref, scale_ref, *rest, mode, nw, ns, nj, n_tiles, has_extra, cast, copy):
    n_w = 2 if mode == "swiglu" else 1
    rest = list(rest)
    w_refs = [rest.pop(0) for _ in range(n_w)]
    cs_ref = rest.pop(0) if mode == "linear" else None
    we_ref = rest.pop(0) if has_extra else None
    src_ref = rest.pop(0) if copy else None
    srce_ref = rest.pop(0) if copy and has_extra else None
    out_ref = rest.pop(0)
    oute_ref = rest.pop(0) if has_extra else None
    wq_refs = [rest.pop(0) for _ in range(n_w)] if cast else None
    h_scr, gm_scr, sh_scr = rest
    t = pl.program_id(0)
    i, j, _, s = _pipe_decode(t, nw, ns, nj, n_tiles)
    running = t >= nw
    d = x_ref.shape[1]
    sr = x_ref.shape[0]

    @pl.when(jnp.logical_or(t == 0, jnp.logical_and(running, j == 0)))
    def _():
        gm_scr[...] = jnp.broadcast_to(gain_ref[...] * (1.0 + scale_ref[0]), (V7X_SUBLANES, d))
        sh_scr[...] = jnp.broadcast_to(shift_ref[0], (V7X_SUBLANES, d))

    def normalise_slice(h_dst):
        reps = NORM_ROWS // V7X_SUBLANES
        gm = jnp.concatenate([gm_scr[...]] * reps, axis=0)
        sh = jnp.concatenate([sh_scr[...]] * reps, axis=0)
        for r0 in range(0, sr, NORM_ROWS):
            x = x_ref[r0:r0 + NORM_ROWS, :]
            rs = lax.rsqrt(jnp.mean(x * x, axis=-1, keepdims=True) + EPS)
            rows = pl.ds(pl.multiple_of(s * sr + r0, NORM_ROWS), NORM_ROWS)
            h_dst[rows, :] = (x * rs * gm + sh).astype(BF16)

    def weight(k):
        if not cast:
            return w_refs[k]
        wq_refs[k][...] = w_refs[k][...].astype(BF16)
        return wq_refs[k]

    def step(parity, with_extra):
        h_first = h_scr[parity]
        if mode == "swiglu":
            a = jnp.dot(h_first, weight(0)[...], preferred_element_type=F32)
        else:
            w = weight(0)
            half = w.shape[1] // 2
            a = jnp.dot(h_first, w[:, :half], preferred_element_type=F32)
        if n_tiles > 1:
            normalise_slice(h_scr.at[1 - parity])
        h_second = h_scr[i % 2]
        if mode == "swiglu":
            b = jnp.dot(h_second, weight(1)[...], preferred_element_type=F32)
            out_ref[...] = (a * _sigmoid(a) * b).astype(out_ref.dtype)
        else:
            b = jnp.dot(h_second, w[:, half:], preferred_element_type=F32)
            out_ref[:, :half] = (a * cs_ref[:, :half]).astype(out_ref.dtype)
            out_ref[:, half:] = (b * cs_ref[:, half:]).astype(out_ref.dtype)
        if with_extra:
            oute_ref[...] = jnp.dot(h_second, we_ref[...], preferred_element_type=F32)

    @pl.when(t < ns)
    def _():
        normalise_slice(h_scr.at[0])

    if copy:
        @pl.when(t < nj)
        def _():
            out_ref[...] = src_ref[...]

        if has_extra:
            @pl.when(t == 0)
            def _():
                oute_ref[...] = srce_ref[...]

    for parity in (0, 1):
        here = jnp.logical_and(running, i % 2 == parity)
        if has_extra:
            pl.when(jnp.logical_and(here, j == 0))(functools.partial(step, parity, True))
            pl.when(jnp.logical_and(here, j > 0))(functools.partial(step, parity, False))
        else:
            pl.when(here)(functools.partial(step, parity, False))


def _pnm_call(x, gain, shift, scale, mode, weights, n, tn, col_scale, out_dtype, extra, geom,
              tile0, n_tiles, cast, tile0_outs):
    m, d = x.shape
    tm = 2 * ROW_TILE
    nj = n // tn
    ns = 1
    while 2 * ns <= min(nj, 8):
        ns *= 2
    sr = tm // ns
    assert sr % NORM_ROWS == 0
    copy = bool(tile0_outs)
    assert tile0 == (1 if copy else 0)
    nw = max(ns, nj) if copy else ns
    dec = functools.partial(_pipe_decode, nw=nw, ns=ns, nj=nj, n_tiles=n_tiles)

    def out_block(t):
        i, j = dec(t)[:2]
        if not copy:
            return i, j
        return jnp.where(t < nw, 0, 1 + i), jnp.where(t < nw, jnp.minimum(t, nj - 1), j)
    grp = functools.partial(_group_of_tile, tm=tm, **geom)
    mod_spec = pl.BlockSpec((1, 1, d), lambda t: (grp(tile0 + dec(t)[2]), 0, 0))
    in_specs = [
        pl.BlockSpec((sr, d), lambda t: ((tile0 + dec(t)[2]) * ns + dec(t)[3], 0)),
        pl.BlockSpec((1, d), lambda t: (0, 0)),
        mod_spec, mod_spec,
    ]
    args = [x, gain, shift, scale]
    for w, lead, col0 in weights:
        assert col0 % tn == 0
        in_specs.append(pl.BlockSpec((None,) * len(lead) + (d, tn),
                                     lambda t, lead=lead, c0=col0 // tn: tuple(lead) + (0, dec(t)[1] + c0)))
        args.append(w)
    if mode == "linear":
        in_specs.append(pl.BlockSpec((1, tn), lambda t: (0, dec(t)[1])))
        args.append(col_scale)
    if extra is not None:
        in_specs.append(pl.BlockSpec(extra.shape, lambda t: (0, 0)))
        args.append(extra)
    if copy:
        in_specs.append(pl.BlockSpec((tm, tn), lambda t: (0, jnp.minimum(t, nj - 1))))
        args.append(tile0_outs[0])
        if extra is not None:
            in_specs.append(pl.BlockSpec((tm, extra.shape[1]), lambda t: (0, 0)))
            args.append(tile0_outs[1])
    out_rows = (tile0 + n_tiles) * tm
    out_specs = [pl.BlockSpec((tm, tn), out_block)]
    out_shape = [jax.ShapeDtypeStruct((out_rows, n), out_dtype)]
    if extra is not None:
        out_specs.append(pl.BlockSpec((tm, extra.shape[1]), lambda t: (out_block(t)[0], 0)))
        out_shape.append(jax.ShapeDtypeStruct((out_rows, extra.shape[1]), F32))
    if cast:
        for _ in weights:
            out_specs.append(pl.BlockSpec((d, tn), lambda t: (0, dec(t)[1])))
            out_shape.append(jax.ShapeDtypeStruct((d, n), BF16))
    return pl.pallas_call(
        functools.partial(_pnm_kernel, mode=mode, nw=nw, ns=ns, nj=nj, n_tiles=n_tiles,
                          has_extra=extra is not None, cast=cast, copy=copy),
        grid=(nw + n_tiles * nj,),
        in_specs=in_specs,
        out_specs=out_specs,
        out_shape=out_shape,
        scratch_shapes=[pltpu.VMEM((2, tm, d), BF16),
                        pltpu.VMEM((V7X_SUBLANES, d), F32), pltpu.VMEM((V7X_SUBLANES, d), F32)],
        compiler_params=_params("arbitrary"),
        name="norm_" + mode + ("_head" if cast else ""),
    )(*args)


def _pipelined_norm_matmul(x, gain, shift, scale, mode, weights, n, tn, col_scale, out_dtype, extra, geom):
    n_tiles = x.shape[0] // (2 * ROW_TILE)
    n_main = 2 if extra is not None else 1
    head = _pnm_call(x, gain, shift, scale, mode, weights, n, tn // 2, col_scale, out_dtype, extra, geom,
                     0, 1, True, [])
    outs, rounded = head[:n_main], head[n_main:]
    if n_tiles > 1:
        outs = _pnm_call(x, gain, shift, scale, mode, [(w, (), 0) for w in rounded], n, tn, col_scale, out_dtype,
                         extra, geom, 1, n_tiles - 1, False, list(outs))
    return outs[0] if extra is None else tuple(outs)


def _mr_kernel(h_ref, w_ref, x_ref, gate_ref, out_ref):
    acc = jnp.dot(h_ref[...], w_ref[...], preferred_element_type=F32)
    out_ref[...] = x_ref[...] + gate_ref[0] * acc


def _matmul_residual(h, w, lead, x, gate, row0, n_rows, geom):
    k = h.shape[1]
    d = x.shape[1]
    tm = 2 * ROW_TILE
    tn = _col_tile(d, 512)
    assert row0 % tm == 0 and n_rows % tm == 0
    t0 = row0 // tm
    grp = functools.partial(_group_of_tile, tm=tm, **geom)
    return pl.pallas_call(
        _mr_kernel,
        grid=(n_rows // tm, d // tn),
        in_specs=[
            pl.BlockSpec((tm, k), lambda i, j: (i, 0)),
            _weight_spec(lead, k, tn),
            pl.BlockSpec((tm, tn), lambda i, j: (i + t0, j)),
            pl.BlockSpec((1, 1, tn), lambda i, j: (grp(i + t0), 0, j)),
        ],
        out_specs=pl.BlockSpec((tm, tn), lambda i, j: (i + t0, j)),
        out_shape=jax.ShapeDtypeStruct(x.shape, F32),
        input_output_aliases={2: 0},
        compiler_params=_params("arbitrary", "arbitrary"),
        name="matmul_residual",
    )(h, w, x, gate)


def _rms_kernel(x_ref, gain_ref, out_ref):
    x = x_ref[...]
    ms = jnp.mean(x * x, axis=-1, keepdims=True)
    out_ref[...] = x * lax.rsqrt(ms + EPS) * gain_ref[...]


def _final_rms(x, gain, n_rows):
    d = x.shape[1]
    tm = 256
    return pl.pallas_call(
        _rms_kernel,
        grid=(n_rows // tm,),
        in_specs=[pl.BlockSpec((tm, d), lambda i: (i, 0)), pl.BlockSpec((1, d), lambda i: (0, 0))],
        out_specs=pl.BlockSpec((tm, d), lambda i: (i, 0)),
        out_shape=jax.ShapeDtypeStruct((n_rows, d), F32),
        compiler_params=_params("arbitrary"),
        name="final_rms",
    )(x, gain)


def _na_table_kernel(rpb_ref, dc_ref, mask_ref, out_ref, *, rows, n_dr, n_dc):
    w = GRID_W
    head = pl.program_id(0)
    dc = dc_ref[...]
    inside = mask_ref[...] > 0
    hits = [dc == c for c in range(n_dc)]
    outside = jnp.full((w, w), NEG_BIG, F32)
    blocks = []
    for dr in range(n_dr):
        acc = jnp.zeros((w, w), F32)
        for c in range(n_dc):
            acc = jnp.where(hits[c], rpb_ref[(head * n_dr + dr) * n_dc + c], acc)
        blocks.append(jnp.where(inside, acc, NEG_BIG))
    g, u, kh = NA_ROW_GROUP, NA_UNION, min(NA_MAX_KH, rows)
    for vi, (rg, u0) in enumerate(((0, 0), (g, 0), (rows - g, rows - u))):
        for gi in range(g):
            r = rg + gi
            start = min(max(r - kh // 2, 0), rows - kh)
            row = [blocks[u0 + j - r + NA_MAX_KH - 1] if start <= u0 + j < start + kh else outside
                   for j in range(u)]
            out_ref[0, vi, gi * w:(gi + 1) * w, :] = jnp.concatenate(row, axis=-1)


def _na_bias_tables(rpb, rows):
    n_heads, n_dr, n_dc = rpb.shape
    w = GRID_W
    col = np.arange(w)
    col_start = np.clip(col - NA_KW // 2, 0, w - NA_KW)
    col_mask = (col[None] >= col_start[:, None]) & (col[None] < col_start[:, None] + NA_KW)
    dc_idx = np.clip(col[None] - col[:, None], -(NA_KW - 1), NA_KW - 1) + NA_KW - 1
    shape = (n_heads, 3, NA_ROW_GROUP * w, NA_UNION * w)
    return pl.pallas_call(
        functools.partial(_na_table_kernel, rows=rows, n_dr=n_dr, n_dc=n_dc),
        grid=(n_heads,),
        in_specs=[
            pl.BlockSpec(memory_space=pltpu.SMEM),
            pl.BlockSpec((w, w), lambda h: (0, 0)),
            pl.BlockSpec((w, w), lambda h: (0, 0)),
        ],
        out_specs=pl.BlockSpec((1,) + shape[1:], lambda h: (h, 0, 0, 0)),
        out_shape=jax.ShapeDtypeStruct(shape, F32),
        compiler_params=_params("arbitrary"),
        name="na_bias_tables",
    )(rpb.reshape(-1), jnp.asarray(dc_idx, jnp.int32), jnp.asarray(col_mask, jnp.int32))


def _na_kernel(q_ref, k_ref, v_ref, qc_ref, kc_ref, vc_ref, bias_ref, o_ref, oc_ref, *, rows):
    w = GRID_W
    gq = NA_ROW_GROUP * w
    gk = NA_UNION * w
    n_groups = rows // NA_ROW_GROUP
    nt = (((1,), (1,)), ((), ()))
    kc = kc_ref[...]
    vc = vc_ref[...]

    def body(gi, carry):
        variant = jnp.where(gi == 0, 0, jnp.where(gi == n_groups - 1, 2, 1))
        u0 = jnp.clip(gi * NA_ROW_GROUP - NA_MAX_KH // 2, 0, rows - NA_UNION)
        q_rows = pl.ds(pl.multiple_of(gi * gq, gq), gq)
        k_rows = pl.ds(pl.multiple_of(u0 * w, w), gk)
        qg = q_ref[q_rows, :]
        s_lat = lax.dot_general(qg, k_ref[k_rows, :], nt, preferred_element_type=F32) + bias_ref[0, variant]
        s_ctx = lax.dot_general(qg, kc, nt, preferred_element_type=F32)
        m = jnp.maximum(jnp.max(s_lat, axis=-1, keepdims=True), jnp.max(s_ctx, axis=-1, keepdims=True))
        p_lat = jnp.exp(s_lat - m)
        p_ctx = jnp.exp(s_ctx - m)
        denom = jnp.sum(p_lat, axis=-1, keepdims=True) + jnp.sum(p_ctx, axis=-1, keepdims=True)
        o = jnp.dot(p_lat.astype(BF16), v_ref[k_rows, :], preferred_element_type=F32)
        o = o + jnp.dot(p_ctx.astype(BF16), vc, preferred_element_type=F32)
        o_ref[q_rows, :] = (o / denom).astype(o_ref.dtype)
        return carry

    lax.fori_loop(0, n_groups, body, 0, unroll=4)

    s_c = lax.dot_general(qc_ref[...], kc, nt, preferred_element_type=F32)
    p_c = jnp.exp(s_c - jnp.max(s_c, axis=-1, keepdims=True))
    o_c = jnp.dot(p_c.astype(BF16), vc, preferred_element_type=F32)
    oc_ref[...] = (o_c / jnp.sum(p_c, axis=-1, keepdims=True)).astype(oc_ref.dtype)


def _na_attention(qkv, bias_tables, n_batch, seq, ctx_len):
    d = qkv.shape[1] // 3
    n_heads = d // NA_HEAD_DIM
    dh = NA_HEAD_DIM
    rows = seq // GRID_W
    assert rows % NA_ROW_GROUP == 0 and rows >= NA_UNION
    lat_blocks = n_batch * seq // ctx_len
    lat = lambda part: pl.BlockSpec((seq, dh), lambda b, h: (b, part * n_heads + h))
    ctx = lambda part: pl.BlockSpec((ctx_len, dh), lambda b, h: (lat_blocks + b, part * n_heads + h))
    o_lat, o_ctx = pl.pallas_call(
        functools.partial(_na_kernel, rows=rows),
        grid=(n_batch, n_heads),
        in_specs=[lat(0), lat(1), lat(2), ctx(0), ctx(1), ctx(2),
                  pl.BlockSpec((1,) + bias_tables.shape[1:], lambda b, h: (h, 0, 0, 0))],
        out_specs=[pl.BlockSpec((seq, dh), lambda b, h: (b, h)),
                   pl.BlockSpec((ctx_len, dh), lambda b, h: (b, h))],
        out_shape=[jax.ShapeDtypeStruct((n_batch * seq, d), BF16),
                   jax.ShapeDtypeStruct((n_batch * ctx_len, d), BF16)],
        compiler_params=_params("arbitrary", "arbitrary"),
        name="na_attention",
    )(qkv, qkv, qkv, qkv, qkv, qkv, bias_tables)
    return o_lat, o_ctx


def _split3(x):
    x1 = x.astype(BF16)
    r1 = x - x1.astype(F32)
    x2 = r1.astype(BF16)
    x3 = (r1 - x2.astype(F32)).astype(BF16)
    return x1, x2, x3


def _gla_prep_kernel(q_ref, k_ref, gl_ref, gup_ref, gbias_ref, cos_ref, sin_ref, tril_ref, triu_ref,
                     qo_ref, ko_ref, bf_ref, bb_ref):
    cos = cos_ref[...]
    sin = sin_ref[...]

    def rope(x):
        halves = [pltpu.roll(x[:, s:s + V7X_LANES], V7X_LANES // 2, 1)
                  for s in range(0, x.shape[1], V7X_LANES)]
        return x * cos + jnp.concatenate(halves, axis=1) * sin

    qo_ref[...] = rope(q_ref[...])
    ko_ref[...] = rope(k_ref[...])

    gl = gl_ref[...].astype(BF16)
    tri_rows = tril_ref.shape[0]
    for direction, (tri_ref, out_ref) in enumerate(((tril_ref, bf_ref), (triu_ref, bb_ref))):
        pre = jnp.dot(gl, gup_ref[direction].astype(BF16), preferred_element_type=F32) + gbias_ref[direction]
        g = (jnp.minimum(pre, 0.0) - jnp.log(1.0 + jnp.exp(-jnp.abs(pre)))) * (LOG2_E / GLA_GATE_NORM)
        tri = tri_ref[...]
        for r0 in range(0, g.shape[0], tri_rows):
            parts = _split3(g[r0:r0 + tri_rows])
            out_ref[r0:r0 + tri_rows, :] = sum(jnp.dot(tri, p, preferred_element_type=F32) for p in parts)


def _gla_prep(qk, gl, gate_up_padded, gate_bias, cos_tab, sin_tab):
    m = qk.shape[0]
    hk = qk.shape[1] // 2
    dk = hk // GLA_HEADS
    tm = ROW_TILE
    tri_rows = 256
    c = GLA_CHUNK
    idx = np.arange(tri_rows)
    same = (idx[:, None] // c) == (idx[None] // c)
    tril = jnp.asarray(same & (idx[None] <= idx[:, None]), BF16)
    triu = jnp.asarray(same & (idx[None] >= idx[:, None]), BF16)
    head = lambda off: pl.BlockSpec((tm, dk), lambda i, h: (i, off + h))
    out_sds = jax.ShapeDtypeStruct((m, hk), F32)
    return pl.pallas_call(
        _gla_prep_kernel,
        grid=(m // tm, GLA_HEADS),
        in_specs=[
            head(0), head(GLA_HEADS),
            pl.BlockSpec((tm, V7X_LANES), lambda i, h: (i, 0)),
            pl.BlockSpec((2, V7X_LANES, dk), lambda i, h: (0, 0, h)),
            pl.BlockSpec((2, 1, dk), lambda i, h: (0, 0, h)),
            pl.BlockSpec((tm, dk), lambda i, h: (i, 0)),
            pl.BlockSpec((tm, dk), lambda i, h: (i, 0)),
            pl.BlockSpec((tri_rows, tri_rows), lambda i, h: (0, 0)),
            pl.BlockSpec((tri_rows, tri_rows), lambda i, h: (0, 0)),
        ],
        out_specs=[head(0)] * 4,
        out_shape=[out_sds] * 4,
        compiler_params=_params("arbitrary", "arbitrary"),
        name="gla_prep",
    )(qk, qk, gl, gate_up_padded, gate_bias, cos_tab, sin_tab, tril, triu)


def _gla_scan_kernel(q_ref, k_ref, b_ref, v_ref, *rest, reverse, finish, heads):
    if finish:
        other_ref, r_ref, gain_ref, o_ref, st_scr = rest
    else:
        o_ref, st_scr = rest
    c = GLA_CHUNK
    sub = GLA_SUB
    dk = q_ref.shape[1] // heads
    dv = v_ref.shape[1] // heads
    nt = (((1,), (1,)), ((), ()))
    tn = (((0,), (0,)), ((), ()))

    @pl.when(pl.program_id(2) == 0)
    def _():
        st_scr[...] = jnp.zeros(st_scr.shape, F32)

    row = lax.broadcasted_iota(jnp.int32, (c, c), 0)
    lane = lax.broadcasted_iota(jnp.int32, (c, c), 1)
    t_idx, s_idx = (lane, row) if reverse else (row, lane)
    level_masks = []
    for half in (c // 2, c // 4, c // 8):
        level_masks.append((row // (2 * half) == lane // (2 * half))
                           & (t_idx % (2 * half) >= half) & (s_idx % (2 * half) < half))
    sub_row = lax.broadcasted_iota(jnp.int32, (sub, 1), 0)
    sub_lane = lax.broadcasted_iota(jnp.int32, (sub, c), 1)

    n_chunks = q_ref.shape[0] // c
    order = range(n_chunks - 1, -1, -1) if reverse else range(n_chunks)
    def chunk(ci, hh):
        rows = slice(ci * c, (ci + 1) * c)
        kcols = slice(hh * dk, (hh + 1) * dk)
        vcols = slice(hh * dv, (hh + 1) * dv)
        q = q_ref[rows, kcols]
        k = k_ref[rows, kcols]
        b = b_ref[rows, kcols]
        v = v_ref[rows, vcols]
        b_end = b[0:1, :] if reverse else b[c - 1:c, :]

        st = st_scr[hh]
        o = lax.dot_general((q * jnp.exp2(b)).astype(BF16), st.astype(BF16), nt, preferred_element_type=F32)
        k_dec = (k * jnp.exp2(b_end - b)).astype(BF16)
        st_scr[hh] = st * jnp.exp2(b_end) + lax.dot_general(v, k_dec, tn, preferred_element_type=F32)

        a = jnp.zeros((c, c), F32)
        for half, mask in zip((c // 2, c // 4, c // 8), level_masks):
            anchors = []
            for g0 in range(0, c, 2 * half):
                r = g0 + half if reverse else g0 + half - 1
                anchors.append(jnp.broadcast_to(b[r:r + 1, :], (2 * half, dk)))
            anc = anchors[0] if len(anchors) == 1 else jnp.concatenate(anchors, axis=0)
            e_l = jnp.exp2(-jnp.abs(b - anc))
            a = jnp.where(mask, lax.dot_general((q * e_l).astype(BF16), (k * e_l).astype(BF16), nt,
                                                preferred_element_type=F32), a)

        a_rows = []
        for i in range(c // sub):
            blk = slice(i * sub, (i + 1) * sub)
            q_i = q[blk, :]
            b_i = b[blk, :]
            a_i = a[blk, :]
            for s in range(sub):
                j = i * sub + s
                p = q_i * k[j:j + 1, :] * jnp.exp2(b_i - b[j:j + 1, :])
                col = jnp.sum(p, axis=-1, keepdims=True)
                valid = (sub_row <= s) if reverse else (sub_row >= s)
                a_i = jnp.where((sub_lane == j) & valid, col, a_i)
            a_rows.append(a_i)
        a = jnp.concatenate(a_rows, axis=0)

        o = o + jnp.dot(a.astype(BF16), v, preferred_element_type=F32)
        if finish:
            o = o + other_ref[rows, vcols]
            y = o * lax.rsqrt(jnp.mean(o * o, axis=-1, keepdims=True) + EPS) * gain_ref[...]
            r = r_ref[rows, vcols].astype(F32)
            o = y * (r * _sigmoid(r))
        o_ref[rows, vcols] = o.astype(o_ref.dtype)

    for ci in order:
        for hh in range(heads):
            chunk(ci, hh)


def _gla_scan(q, k, bcum, vr, n_batch, seq, ctx_len, reverse, other=None, head_gain=None):
    finish = other is not None
    m = q.shape[0]
    dk = q.shape[1] // GLA_HEADS
    dv = 2 * dk
    seg = GLA_SEG
    assert seq % seg == 0 and ctx_len % seg == 0
    n_lat, n_ctx = seq // seg, ctx_len // seg
    lat_blocks = n_batch * n_lat

    def row_block(b, s):
        if reverse:
            return jnp.where(s < n_ctx, lat_blocks + b * n_ctx + (n_ctx - 1 - s),
                             b * n_lat + (n_lat - 1 - (s - n_ctx)))
        return jnp.where(s < n_ctx, lat_blocks + b * n_ctx + s, b * n_lat + (s - n_ctx))

    hp = GLA_HEADS_PER_STEP
    assert GLA_HEADS % hp == 0
    n_hb = GLA_HEADS // hp
    qk_spec = pl.BlockSpec((seg, hp * dk), lambda b, h, s: (row_block(b, s), h))
    v_spec = pl.BlockSpec((seg, hp * dv), lambda b, h, s: (row_block(b, s), h))
    in_specs = [qk_spec, qk_spec, qk_spec, v_spec]
    args = [q, k, bcum, vr]
    if finish:
        in_specs += [v_spec,
                     pl.BlockSpec((seg, hp * dv), lambda b, h, s: (row_block(b, s), n_hb + h)),
                     pl.BlockSpec((1, dv), lambda b, h, s: (0, 0))]
        args += [other, vr, head_gain]
    return pl.pallas_call(
        functools.partial(_gla_scan_kernel, reverse=reverse, finish=finish, heads=hp),
        grid=(n_batch, n_hb, n_ctx + n_lat),
        in_specs=in_specs,
        out_specs=v_spec,
        out_shape=jax.ShapeDtypeStruct((m, GLA_HEADS * dv), BF16 if finish else F32),
        scratch_shapes=[pltpu.VMEM((hp, dv, dk), F32)],
        compiler_params=_params("arbitrary", "arbitrary", "arbitrary"),
        name="gla_scan_bwd" if reverse else "gla_scan_fwd",
    )(*args)


def _rope_tables(n_batch, seq, ctx_len, dk):
    half = dk // 2
    inv = ROPE_BASE ** (-jnp.arange(0, half, 2, dtype=F32) / half)
    t = jnp.arange(seq)

    def tabs(pos):
        ang = pos.astype(F32)[:, None] * inv[None]
        cos, sin = jnp.cos(ang), jnp.sin(ang)
        return jnp.concatenate([cos, cos], axis=-1), jnp.concatenate([-sin, sin], axis=-1)

    cos_r, sin_r = tabs(t // GRID_W)
    cos_c, sin_c = tabs(t % GRID_W)
    cos = jnp.tile(jnp.concatenate([cos_r, cos_c], axis=-1), (n_batch, 1))
    sin = jnp.tile(jnp.concatenate([sin_r, sin_c], axis=-1), (n_batch, 1))
    n_ctx = n_batch * ctx_len
    cos = jnp.concatenate([cos, jnp.ones((n_ctx, dk), F32)], axis=0)
    sin = jnp.concatenate([sin, jnp.zeros((n_ctx, dk), F32)], axis=0)
    return cos, sin


def kernel(x, c, ctx, c_ctx, ada_down, ada_up, ada_bias, norm_gain, ffn_w_in, ffn_w_out, na_w_qkv, na_w_o, na_rpb, gla_w_in, gla_w_o, gla_gate_down, gla_gate_up, gla_gate_bias, gla_head_gain, final_gain):
    n_batch, seq, d = x.shape
    ctx_len = ctx.shape[1]
    depth = ada_down.shape[0]
    lat_rows = n_batch * seq
    m = lat_rows + n_batch * ctx_len
    geom = dict(lat_rows=lat_rows, seq=seq, n_batch=n_batch)
    assert seq % (2 * ROW_TILE) == 0 and (n_batch * ctx_len) % (2 * ROW_TILE) == 0
    n_groups = n_batch + 1
    g8 = -(-n_groups // V7X_SUBLANES) * V7X_SUBLANES

    xs = jnp.concatenate([x.reshape(lat_rows, d), ctx.reshape(n_batch * ctx_len, d)], axis=0)

    cvec = jnp.concatenate([c, c_ctx[None], jnp.zeros((g8 - n_groups, d), F32)], axis=0)
    mods = _ada_modulation(cvec, ada_down, ada_up, ada_bias)
    mods = mods.reshape(depth, N_MOD, g8, 1, d)

    hk = GLA_HEADS * (d // 2 // GLA_HEADS)
    dk = hk // GLA_HEADS
    ones_d = jnp.ones((1, 2 * d), F32)
    cos_tab, sin_tab = _rope_tables(n_batch, seq, ctx_len, dk)

    ffn_w_out, na_w_o, gla_w_o = ffn_w_out.astype(BF16), na_w_o.astype(BF16), gla_w_o.astype(BF16)

    for i in range(depth):
        last = i == depth - 1
        mod = mods[i]
        j = i // N_MIXERS
        n_rows = lat_rows if last else m

        def ffn(xs, which, n_rows):
            base = 0 if which == 0 else 6
            f = ffn_w_in.shape[-1] // 2
            tn = _col_tile(f, 512)
            hid = _pipelined_norm_matmul(xs, norm_gain[i, 2 * which][None], mod[base], mod[base + 1], "swiglu",
                                         [(ffn_w_in, (i, which), 0), (ffn_w_in, (i, which), f)],
                                         f, tn, None, BF16, None, geom)
            return _matmul_residual(hid, ffn_w_out, (i, which), xs, 0.5 * mod[base + 2], 0, n_rows, geom)

        xs = ffn(xs, 0, m)

        gain = norm_gain[i, 1][None]
        if i % N_MIXERS == 0:
            col_scale = jnp.concatenate([jnp.full((1, d), NA_HEAD_DIM ** -0.5, F32), jnp.ones((1, 2 * d), F32)], axis=1)
            qkv = _pipelined_norm_matmul(xs, gain, mod[3], mod[4], "linear", [(na_w_qkv, (j,), 0)],
                                         3 * d, _col_tile(3 * d, 1024), col_scale, BF16, None, geom)
            tables = _na_bias_tables(na_rpb[j], seq // GRID_W)
            o_lat, o_ctx = _na_attention(qkv, tables, n_batch, seq, ctx_len)
            xs = _matmul_residual(o_lat, na_w_o, (j,), xs, mod[5], 0, lat_rows, geom)
            if not last:
                xs = _matmul_residual(o_ctx, na_w_o, (j,), xs, mod[5], lat_rows, m - lat_rows, geom)
        else:
            qk_scale = jnp.concatenate([jnp.full((1, hk), dk ** -0.5, F32), jnp.ones((1, hk), F32)], axis=1)
            rank = GLA_GATE_RANK
            gdown = jnp.concatenate([gla_gate_down[j, 0], gla_gate_down[j, 1],
                                     jnp.zeros((d, V7X_LANES - 2 * rank), F32)], axis=1)
            tn = _col_tile(2 * hk, 1024)
            qk, gl = _pipelined_norm_matmul(xs, gain, mod[3], mod[4], "linear", [(gla_w_in, (j,), 0)],
                                            2 * hk, tn // 2, qk_scale, F32, gdown.astype(BF16), geom)
            vr = _pipelined_norm_matmul(xs, gain, mod[3], mod[4], "linear", [(gla_w_in, (j,), 2 * hk)],
                                        2 * d, tn, ones_d, BF16, None, geom)
            gup = jnp.zeros((2, V7X_LANES, hk), F32)
            gup = gup.at[0, :rank].set(gla_gate_up[j, 0]).at[1, rank:2 * rank].set(gla_gate_up[j, 1])
            q_r, k_r, b_f, b_b = _gla_prep(qk, gl, gup, gla_gate_bias[j][:, None, :], cos_tab, sin_tab)
            o_f = _gla_scan(q_r, k_r, b_f, vr, n_batch, seq, ctx_len, reverse=False)
            mixed = _gla_scan(q_r, k_r, b_b, vr, n_batch, seq, ctx_len, reverse=True,
                              other=o_f, head_gain=gla_head_gain[j][None])
            xs = _matmul_residual(mixed, gla_w_o, (j,), xs, mod[5], 0, n_rows, geom)

        xs = ffn(xs, 1, n_rows)

    return _final_rms(xs, final_gain[None], lat_rows).reshape(n_batch, seq, d)
```
